```python
import math
import jax, jax.numpy as jnp
from jax import lax
import numpy as np

D_MODEL = 1024
BATCH = 4
SEQ = 4096
DEPTH = 1
DEC_BATCH = 32
DEC_SEQ = 4
PAST_LEN = 16384
PAGE_SIZE = 128

DA_HEADS = 4
DA_HEAD_DIM = 64
DA_QK_WIDTH = DA_HEADS * 2 * DA_HEAD_DIM
DA_V_DIM = 2 * DA_HEAD_DIM
DA_V_WIDTH = DA_HEADS * DA_V_DIM
SB_HEADS = 8
SB_HEAD_DIM = 64
SB_WIDTH = SB_HEADS * SB_HEAD_DIM
ROPE_THETA = 500000.0
ROPE_DIM = DA_HEAD_DIM // 4
D_FF = 2816
FFN_RES = 0.5
Q_BLOCK = 128
RMS_EPS = 1e-6
NEG_INF = -1e30
SPLIT_SIZES = (DA_QK_WIDTH, DA_QK_WIDTH, DA_V_WIDTH, SB_WIDTH, SB_WIDTH, SB_WIDTH, D_MODEL, D_MODEL)
IN_COLS = 2 * DA_QK_WIDTH + DA_V_WIDTH + 3 * SB_WIDTH + 2 * D_MODEL

kernel_name = 'diff_stickbreak_gated_macaron_step'


def rmsnorm(x, g):
    xf = x.astype(jnp.float32)
    y = xf * lax.rsqrt(jnp.mean(xf * xf, axis=-1, keepdims=True) + RMS_EPS)
    return (y * g.astype(jnp.float32)).astype(x.dtype)


def swiglu(x, wg, wu, wd):
    return (jax.nn.silu(x @ wg) * (x @ wu)) @ wd


def ffn_half(x, g, wg, wu, wd):
    return x + FFN_RES * swiglu(rmsnorm(x, g), wg, wu, wd)


def rope(x, pos):
    half = ROPE_DIM // 2
    inv_freq = ROPE_THETA ** (-jnp.arange(0, ROPE_DIM, 2, dtype=jnp.float32) / ROPE_DIM)
    ang = pos.astype(jnp.float32)[:, None] * inv_freq[None, :]
    cos = jnp.cos(ang)[None, :, None, None, :]
    sin = jnp.sin(ang)[None, :, None, None, :]
    xf = x.astype(jnp.float32)
    x1 = xf[..., :half]
    x2 = xf[..., half:ROPE_DIM]
    out = jnp.concatenate([x1 * cos - x2 * sin, x2 * cos + x1 * sin, xf[..., ROPE_DIM:]], axis=-1)
    return out.astype(x.dtype)


def split_projection(z):
    B, T = z.shape[:2]
    idx = np.cumsum(SPLIT_SIZES)[:-1].tolist()
    qa, ka, va, qb, kb, vb, ga, gb = jnp.split(z, idx, axis=-1)
    qa = qa.reshape(B, T, DA_HEADS, 2, DA_HEAD_DIM)
    ka = ka.reshape(B, T, DA_HEADS, 2, DA_HEAD_DIM)
    va = va.reshape(B, T, DA_HEADS, DA_V_DIM)
    qb = qb.reshape(B, T, SB_HEADS, SB_HEAD_DIM)
    kb = kb.reshape(B, T, SB_HEADS, SB_HEAD_DIM)
    vb = vb.reshape(B, T, SB_HEADS, SB_HEAD_DIM)
    return qa, ka, va, qb, kb, vb, ga, gb


def diff_lambda(lq1, lk1, lq2, lk2, lam_init):
    f = jnp.float32
    return (jnp.exp(jnp.sum(lq1.astype(f) * lk1.astype(f)))
            - jnp.exp(jnp.sum(lq2.astype(f) * lk2.astype(f))) + lam_init)


def diff_attn_prompt(q, k, v, lam):
    B, S = q.shape[:2]
    nb = S // Q_BLOCK
    scale = DA_HEAD_DIM ** -0.5
    qblk = jnp.moveaxis(q.reshape(B, nb, Q_BLOCK, DA_HEADS, 2, DA_HEAD_DIM), 1, 0)
    kpos = jnp.arange(S)

    def block(args):
        qi, i = args
        s = jnp.einsum('bqhcd,bkhcd->bhcqk', qi, k).astype(jnp.float32) * scale
        qpos = i * Q_BLOCK + jnp.arange(Q_BLOCK)
        mask = kpos[None, :] <= qpos[:, None]
        p = jax.nn.softmax(jnp.where(mask, s, NEG_INF), axis=-1)
        p = p[:, :, 0] - lam * p[:, :, 1]
        return jnp.einsum('bhqk,bkhe->bqhe', p.astype(v.dtype), v)

    o = lax.map(block, (qblk, jnp.arange(nb)))
    return jnp.moveaxis(o, 0, 1).reshape(B, S, DA_HEADS, DA_V_DIM)


def sb_prompt(q, k, v):
    B, S = q.shape[:2]
    nb = S // Q_BLOCK
    scale = SB_HEAD_DIM ** -0.5
    qblk = jnp.moveaxis(q.reshape(B, nb, Q_BLOCK, SB_HEADS, SB_HEAD_DIM), 1, 0)
    kpos = jnp.arange(S)

    def block(args):
        qi, i = args
        z = jnp.einsum('bqhd,bkhd->bhqk', qi, k).astype(jnp.float32) * scale
        qpos = i * Q_BLOCK + jnp.arange(Q_BLOCK)
        mask = kpos[None, :] < qpos[:, None]
        log_keep = jnp.where(mask, jax.nn.log_sigmoid(-z), 0.0)
        suffix = lax.cumsum(log_keep, axis=3, reverse=True) - log_keep
        a = jnp.where(mask, jnp.exp(jax.nn.log_sigmoid(z) + suffix), 0.0)
        return jnp.einsum('bhqk,bkhd->bqhd', a.astype(v.dtype), v)

    o = lax.map(block, (qblk, jnp.arange(nb)))
    return jnp.moveaxis(o, 0, 1).reshape(B, S, SB_HEADS, SB_HEAD_DIM)


def diff_attn_sample(q, k_new, v_new, cache_k, cache_v, page_table, lam):
    T = q.shape[1]
    scale = DA_HEAD_DIM ** -0.5
    causal = jnp.tril(jnp.ones((T, T), dtype=bool))
    s = jnp.einsum('bqhcd,bkhcd->bhcqk', q, k_new).astype(jnp.float32) * scale
    s = jnp.where(causal, s, NEG_INF)
    m = jnp.max(s, axis=-1)
    p = jnp.exp(s - m[..., None])
    l = jnp.sum(p, axis=-1)
    acc = jnp.einsum('bhcqk,bkhe->bhcqe', p, v_new.astype(jnp.float32))

    def step(carry, pg):
        m, l, acc = carry
        kb = cache_k[pg]
        vb = cache_v[pg]
        s = jnp.einsum('bqhcd,bkhcd->bhcqk', q, kb).astype(jnp.float32) * scale
        m_new = jnp.maximum(m, jnp.max(s, axis=-1))
        corr = jnp.exp(m - m_new)
        p = jnp.exp(s - m_new[..., None])
        l = l * corr + jnp.sum(p, axis=-1)
        acc = acc * corr[..., None] + jnp.einsum('bhcqk,bkhe->bhcqe', p, vb.astype(jnp.float32))
        return (m_new, l, acc), None

    (m, l, acc), _ = lax.scan(step, (m, l, acc), page_table.T)
    o = acc / l[..., None]
    o = o[:, :, 0] - lam * o[:, :, 1]
    return jnp.transpose(o, (0, 2, 1, 3)).astype(q.dtype)


def sb_sample(q, k_new, v_new, cache_k, cache_v, page_table):
    T = q.shape[1]
    scale = SB_HEAD_DIM ** -0.5
    strict = jnp.tril(jnp.ones((T, T), dtype=bool), -1)
    z = jnp.einsum('bqhd,bkhd->bhqk', q, k_new).astype(jnp.float32) * scale
    log_keep = jnp.where(strict, jax.nn.log_sigmoid(-z), 0.0)
    suffix = lax.cumsum(log_keep, axis=3, reverse=True) - log_keep
    a = jnp.where(strict, jnp.exp(jax.nn.log_sigmoid(z) + suffix), 0.0)
    acc = jnp.einsum('bhqk,bkhd->bhqd', a, v_new.astype(jnp.float32))
    carry_log = jnp.sum(log_keep, axis=-1)

    def step(carry, pg):
        c, acc = carry
        kb = cache_k[pg]
        vb = cache_v[pg]
        z = jnp.einsum('bqhd,bkhd->bhqk', q, kb).astype(jnp.float32) * scale
        lk = jax.nn.log_sigmoid(-z)
        suffix = lax.cumsum(lk, axis=3, reverse=True) - lk + c[..., None]
        a = jnp.exp(jax.nn.log_sigmoid(z) + suffix)
        acc = acc + jnp.einsum('bhqk,bkhd->bhqd', a, vb.astype(jnp.float32))
        return (c + jnp.sum(lk, axis=-1), acc), None

    (_, acc), _ = lax.scan(step, (carry_log, acc), page_table.T, reverse=True)
    return jnp.transpose(acc, (0, 2, 1, 3)).astype(q.dtype)


def merge_branches(oa, ob, ga, gb, subln_g, lam_init, w_ba, w_bb, w_o):
    B, T = oa.shape[:2]
    oa = rmsnorm(oa, subln_g) * (1.0 - lam_init)
    ya = oa.reshape(B, T, DA_V_WIDTH) @ w_ba
    yb = ob.reshape(B, T, SB_WIDTH) @ w_bb
    m = jax.nn.sigmoid(ga) * ya + jax.nn.sigmoid(gb) * yb
    return m @ w_o


def setup_inputs(seed: int = 0) -> dict:
    key = jax.random.key(seed)
    ks = jax.random.split(key, 32)
    f = jnp.float32
    N_PAGES = PAST_LEN // PAGE_SIZE
    N_POOL = (DEC_BATCH * N_PAGES * 5) // 4

    def nrm(k, shape, scale):
        return jax.random.normal(k, shape, f) * scale

    def gain(k, shape):
        return 1.0 + 0.01 * jax.random.normal(k, shape, f)

    page_table = jax.random.permutation(ks[6], N_POOL)[:DEC_BATCH * N_PAGES]
    page_table = page_table.reshape(DEC_BATCH, N_PAGES).astype(jnp.int32)
    return {
        'x_prompt': nrm(ks[0], (BATCH, SEQ, D_MODEL), 1.0),
        'x_sample': nrm(ks[1], (DEC_BATCH, DEC_SEQ, D_MODEL), 1.0),
        'cache_diff_k': nrm(ks[2], (DEPTH, N_POOL, PAGE_SIZE, DA_HEADS, 2, DA_HEAD_DIM), 1.0),
        'cache_diff_v': nrm(ks[3], (DEPTH, N_POOL, PAGE_SIZE, DA_HEADS, DA_V_DIM), 1.0),
        'cache_sb_k': nrm(ks[4], (DEPTH, N_POOL, PAGE_SIZE, SB_HEADS, SB_HEAD_DIM), 1.0),
        'cache_sb_v': nrm(ks[5], (DEPTH, N_POOL, PAGE_SIZE, SB_HEADS, SB_HEAD_DIM), 1.0),
        'page_table': page_table,
        'norm1': gain(ks[7], (DEPTH, D_MODEL)),
        'w_ffn1_gate': nrm(ks[8], (DEPTH, D_MODEL, D_FF), D_MODEL ** -0.5),
        'w_ffn1_up': nrm(ks[9], (DEPTH, D_MODEL, D_FF), D_MODEL ** -0.5),
        'w_ffn1_down': nrm(ks[10], (DEPTH, D_FF, D_MODEL), D_FF ** -0.5),
        'norm2': gain(ks[11], (DEPTH, D_MODEL)),
        'w_in': nrm(ks[12], (DEPTH, D_MODEL, IN_COLS), D_MODEL ** -0.5),
        'lambda_q1': nrm(ks[13], (DEPTH, DA_HEAD_DIM), 0.1),
        'lambda_k1': nrm(ks[14], (DEPTH, DA_HEAD_DIM), 0.1),
        'lambda_q2': nrm(ks[15], (DEPTH, DA_HEAD_DIM), 0.1),
        'lambda_k2': nrm(ks[16], (DEPTH, DA_HEAD_DIM), 0.1),
        'subln_g': gain(ks[17], (DEPTH, DA_V_DIM)),
        'w_branch_a': nrm(ks[18], (DEPTH, DA_V_WIDTH, D_MODEL), DA_V_WIDTH ** -0.5),
        'w_branch_b': nrm(ks[19], (DEPTH, SB_WIDTH, D_MODEL), SB_WIDTH ** -0.5),
        'w_out': nrm(ks[20], (DEPTH, D_MODEL, D_MODEL), D_MODEL ** -0.5),
        'norm3': gain(ks[21], (DEPTH, D_MODEL)),
        'w_ffn2_gate': nrm(ks[22], (DEPTH, D_MODEL, D_FF), D_MODEL ** -0.5),
        'w_ffn2_up': nrm(ks[23], (DEPTH, D_MODEL, D_FF), D_MODEL ** -0.5),
        'w_ffn2_down': nrm(ks[24], (DEPTH, D_FF, D_MODEL), D_FF ** -0.5),
        'norm_f': gain(ks[25], (D_MODEL,)),
    }


def reference(x_prompt, x_sample, cache_diff_k, cache_diff_v, cache_sb_k, cache_sb_v, page_table,
              norm1, w_ffn1_gate, w_ffn1_up, w_ffn1_down, norm2, w_in,
              lambda_q1, lambda_k1, lambda_q2, lambda_k2, subln_g,
              w_branch_a, w_branch_b, w_out, norm3, w_ffn2_gate, w_ffn2_up, w_ffn2_down, norm_f):
    xp = x_prompt
    xs = x_sample
    pos_p = jnp.arange(xp.shape[1])
    pos_s = PAST_LEN + jnp.arange(xs.shape[1])
    dkp, dvp, skp, svp, dks, dvs, sks, svs = [], [], [], [], [], [], [], []
    for l in range(DEPTH):
        lam_init = 0.8 - 0.6 * math.exp(-0.3 * l)
        lam = diff_lambda(lambda_q1[l], lambda_k1[l], lambda_q2[l], lambda_k2[l], lam_init)
        xp = ffn_half(xp, norm1[l], w_ffn1_gate[l], w_ffn1_up[l], w_ffn1_down[l])
        xs = ffn_half(xs, norm1[l], w_ffn1_gate[l], w_ffn1_up[l], w_ffn1_down[l])
        hp = rmsnorm(xp, norm2[l])
        qa, ka, va, qb, kb, vb, ga, gb = split_projection(hp @ w_in[l])
        qa = rope(qa, pos_p)
        ka = rope(ka, pos_p)
        oa = diff_attn_prompt(qa, ka, va, lam)
        ob = sb_prompt(qb, kb, vb)
        xp = xp + merge_branches(oa, ob, ga, gb, subln_g[l], lam_init, w_branch_a[l], w_branch_b[l], w_out[l])
        dkp.append(ka); dvp.append(va); skp.append(kb); svp.append(vb)
        hs = rmsnorm(xs, norm2[l])
        qa, ka, va, qb, kb, vb, ga, gb = split_projection(hs @ w_in[l])
        qa = rope(qa, pos_s)
        ka = rope(ka, pos_s)
        oa = diff_attn_sample(qa, ka, va, cache_diff_k[l], cache_diff_v[l], page_table, lam)
        ob = sb_sample(qb, kb, vb, cache_sb_k[l], cache_sb_v[l], page_table)
        xs = xs + merge_branches(oa, ob, ga, gb, subln_g[l], lam_init, w_branch_a[l], w_branch_b[l], w_out[l])
        dks.append(ka); dvs.append(va); sks.append(kb); svs.append(vb)
        xp = ffn_half(xp, norm3[l], w_ffn2_gate[l], w_ffn2_up[l], w_ffn2_down[l])
        xs = ffn_half(xs, norm3[l], w_ffn2_gate[l], w_ffn2_up[l], w_ffn2_down[l])
    y_prompt = rmsnorm(xp, norm_f)
    y_sample = rmsnorm(xs, norm_f)
    return (y_prompt, y_sample,
            jnp.stack(dkp), jnp.stack(dvp), jnp.stack(skp), jnp.stack(svp),
            jnp.stack(dks), jnp.stack(dvs), jnp.stack(sks), jnp.stack(svs))
```

```python
import functools
import math

import jax
import jax.numpy as jnp
from jax import lax
from jax.experimental import pallas as pl
from jax.experimental.pallas import tpu as pltpu

F32 = jnp.float32
BF16 = jnp.bfloat16

DA_HEADS = 4
DA_HEAD_DIM = 64
DA_V_DIM = 2 * DA_HEAD_DIM
SB_HEADS = 8
SB_HEAD_DIM = 64
ROPE_THETA = 500000.0
ROPE_DIM = DA_HEAD_DIM // 4
FFN_RES = 0.5
RMS_EPS = 1e-6
NEG_INF = -1e30

LANES = 128
SUBLANES = 8
VMEM_LIMIT_BYTES = 56 * 1024 * 1024

FFN_TOKEN_TILE = 512
FFN_CHUNK = 1408
PROJ_TOKEN_TILE = 256
ATTN_BLOCK = 256
PAGES_PER_STEP = 8


def _const_spec(shape):
    zeros = (0,) * len(shape)
    return pl.BlockSpec(shape, lambda *_: zeros, pipeline_mode=pl.Buffered(1))


def _log2(n):
    assert n & (n - 1) == 0
    return n.bit_length() - 1


def _rmsnorm(x, g):
    ms = jnp.mean(x * x, axis=-1, keepdims=True)
    return x * lax.rsqrt(ms + RMS_EPS) * g


def _dot(a, b):
    return jnp.dot(a, b, preferred_element_type=F32)


def _dot_nt(a, b):
    return lax.dot_general(a, b, (((1,), (1,)), ((), ())), preferred_element_type=F32)


def _swiglu(xn, wg_ref, wu_ref, wd_ref):
    xb = xn.astype(BF16)
    d_ff = wg_ref.shape[1]
    chunk = FFN_CHUNK if d_ff % FFN_CHUNK == 0 else d_ff
    acc = None
    for c in range(d_ff // chunk):
        lo, hi = c * chunk, (c + 1) * chunk
        hg = _dot(xb, wg_ref[:, lo:hi])
        hu = _dot(xb, wu_ref[:, lo:hi])
        h = (hg * jax.nn.sigmoid(hg) * hu).astype(BF16)
        part = _dot(h, wd_ref[lo:hi, :])
        acc = part if acc is None else acc + part
    return acc


def _lam_from_refs(lq1_ref, lk1_ref, lq2_ref, lk2_ref, lam_init):
    a1 = jnp.sum(lq1_ref[...] * lk1_ref[...], axis=-1, keepdims=True)
    a2 = jnp.sum(lq2_ref[...] * lk2_ref[...], axis=-1, keepdims=True)
    return jnp.exp(a1) - jnp.exp(a2) + lam_init


def _neg_softplus(z):
    return -(jnp.maximum(z, 0.0) + jnp.log1p(jnp.exp(-jnp.abs(z))))


def _suffix_sum_matrix(n):
    j = lax.broadcasted_iota(jnp.int32, (n, n), 0)
    k = lax.broadcasted_iota(jnp.int32, (n, n), 1)
    return jnp.where(j >= k, 1.0, 0.0).astype(BF16)


def _inclusive_suffix_sum(x, u):
    hi = x.astype(BF16)
    r1 = x - hi.astype(F32)
    mid = r1.astype(BF16)
    lo = (r1 - mid.astype(F32)).astype(BF16)
    return _dot(hi, u) + _dot(mid, u) + _dot(lo, u)


def _ffn_half_kernel(x_ref, g_ref, wg_ref, wu_ref, wd_ref, o_ref):
    x = x_ref[...]
    o_ref[...] = x + FFN_RES * _swiglu(_rmsnorm(x, g_ref[...]), wg_ref, wu_ref, wd_ref)


def _ffn_half(x, g, wg, wu, wd, *, tm):
    n, d = x.shape
    d_ff = wg.shape[1]
    row = pl.BlockSpec((tm, d), lambda i: (i, 0))
    return pl.pallas_call(
        _ffn_half_kernel,
        grid=(n // tm,),
        in_specs=[row, _const_spec((1, d)), _const_spec((d, d_ff)), _const_spec((d, d_ff)),
                  _const_spec((d_ff, d))],
        out_specs=row,
        out_shape=jax.ShapeDtypeStruct((n, d), F32),
        compiler_params=pltpu.CompilerParams(
            dimension_semantics=("parallel",), vmem_limit_bytes=VMEM_LIMIT_BYTES),
        name="ffn_half",
    )(x, g, wg, wu, wd)


def _rope_rows(t, cos, sin_next, sin_prev):
    half = ROPE_DIM // 2
    groups = []
    for c in range(0, t.shape[1], LANES):
        x = t[:, c:c + LANES]
        nxt = pltpu.roll(x, LANES - half, axis=1)
        prv = pltpu.roll(x, half, axis=1)
        groups.append(x * cos + nxt * sin_next + prv * sin_prev)
    return jnp.concatenate(groups, axis=1)


def _rope_cols(t, cos_t, sin_t):
    half = ROPE_DIM // 2
    assert half == SUBLANES
    pieces = []
    for r in range(0, t.shape[0], DA_HEAD_DIM):
        x1 = t[r:r + half]
        x2 = t[r + half:r + 2 * half]
        pieces += [x1 * cos_t - x2 * sin_t, x2 * cos_t + x1 * sin_t,
                   t[r + 2 * half:r + DA_HEAD_DIM]]
    return jnp.concatenate(pieces, axis=0)


_QA, _KA, _VA, _QB, _KB, _VB, _GA, _GB = range(8)


def _proj_kernel(x_ref, g_ref, w_ref, wt_ref, cos_ref, sn_ref, sp_ref, cos_t_ref, sin_t_ref,
                 *outs, offs, da_scale, sb_scale, sample):
    xb = _rmsnorm(x_ref[...], g_ref[...]).astype(BF16)

    def seg(i):
        return _dot(xb, w_ref[:, offs[i]:offs[i + 1]])

    def seg_t(i):
        w = offs[_KA + 1] - offs[_KA]
        return _dot_nt(wt_ref[i * w:(i + 1) * w, :], xb)

    qa_o, qb_o, ka_t, va_o, kb_t, vb_t, ga_o, gb_o = outs[:8]
    rope_tabs = (cos_ref[...], sn_ref[...], sp_ref[...])
    qa_o[...] = (_rope_rows(seg(_QA), *rope_tabs) * da_scale).astype(qa_o.dtype)
    qb_o[...] = (seg(_QB) * sb_scale).astype(qb_o.dtype)
    ka = _rope_cols(seg_t(0), cos_t_ref[...], sin_t_ref[...])
    kb = seg_t(1)
    vb = seg_t(2)
    va = seg(_VA)
    ka_t[...] = ka
    va_o[...] = va
    kb_t[...] = kb
    vb_t[...] = vb
    ga_o[...] = seg(_GA)
    gb_o[...] = seg(_GB)
    if sample:
        ka_o, kb_o, vb_o = outs[8:]
        ka_o[...] = _rope_rows(seg(_KA), *rope_tabs)
        kb_o[...] = seg(_KB)
        vb_o[...] = seg(_VB)
    else:
        ka_h, va_h, kb_h, vb_h = outs[8:]
        ka_h[...] = ka.astype(BF16)
        va_h[...] = va.astype(BF16)
        kb_h[...] = kb.astype(BF16)
        vb_h[...] = vb.astype(BF16)


def _proj(x, g, w_in, w_t, tabs_rows, tabs_cols, *, tm, sample):
    n, d = x.shape
    qk_w = DA_HEADS * 2 * DA_HEAD_DIM
    v_w = DA_HEADS * DA_V_DIM
    sb_w = SB_HEADS * SB_HEAD_DIM
    assert qk_w == sb_w and w_t.shape[0] == 3 * qk_w
    widths = (qk_w, qk_w, v_w, sb_w, sb_w, sb_w, d, d)
    assert sum(widths) == w_in.shape[1]
    offs = tuple(sum(widths[:i]) for i in range(len(widths) + 1))
    period = tabs_rows[0].shape[0]
    pos_blocks = period // tm
    n_batch = n // period
    q_dtype = F32 if sample else BF16

    def row(w):
        return pl.BlockSpec((tm, w), lambda i: (i, 0))

    col = pl.BlockSpec((None, qk_w, tm), lambda i: (i // pos_blocks, 0, i % pos_blocks))
    tab_r = pl.BlockSpec((tm, LANES), lambda i: (i % pos_blocks, 0))
    tab_c = pl.BlockSpec((ROPE_DIM // 2, tm), lambda i: (0, i % pos_blocks))

    def rows(w, dt):
        return jax.ShapeDtypeStruct((n, w), dt), row(w)

    def cols(dt):
        return jax.ShapeDtypeStruct((n_batch, qk_w, period), dt), col

    outs = [rows(qk_w, q_dtype), rows(sb_w, q_dtype), cols(F32), rows(v_w, F32), cols(F32),
            cols(F32), rows(d, F32), rows(d, F32)]
    if sample:
        outs += [rows(qk_w, F32), rows(sb_w, F32), rows(sb_w, F32)]
    else:
        outs += [cols(BF16), rows(v_w, BF16), cols(BF16), cols(BF16)]
    kern = functools.partial(_proj_kernel, offs=offs, da_scale=DA_HEAD_DIM ** -0.5,
                             sb_scale=SB_HEAD_DIM ** -0.5, sample=sample)
    return pl.pallas_call(
        kern,
        grid=(n // tm,),
        in_specs=[row(d), _const_spec((1, d)), _const_spec(w_in.shape), _const_spec(w_t.shape),
                  tab_r, tab_r, tab_r, tab_c, tab_c],
        out_specs=[o[1] for o in outs],
        out_shape=[o[0] for o in outs],
        compiler_params=pltpu.CompilerParams(
            dimension_semantics=("parallel",), vmem_limit_bytes=VMEM_LIMIT_BYTES),
        name="in_proj",
    )(x, g, w_in, w_t, *tabs_rows, *tabs_cols)


def _merge_ffn_kernel(x_ref, oa_ref, ob_ref, ga_ref, gb_ref, wa_ref, wb_ref, wo_ref,
                      g3_ref, wg_ref, wu_ref, wd_ref, gf_ref, o_ref):
    ya = _dot(oa_ref[...].astype(BF16), wa_ref[...])
    yb = _dot(ob_ref[...].astype(BF16), wb_ref[...])
    m = jax.nn.sigmoid(ga_ref[...]) * ya + jax.nn.sigmoid(gb_ref[...]) * yb
    x = x_ref[...] + _dot(m.astype(BF16), wo_ref[...])
    y = x + FFN_RES * _swiglu(_rmsnorm(x, g3_ref[...]), wg_ref, wu_ref, wd_ref)
    o_ref[...] = _rmsnorm(y, gf_ref[...])


def _merge_ffn(x, oa, ob, ga, gb, wa, wb, wo, g3, wg, wu, wd, gf, *, tm):
    n, d = x.shape

    def row(w):
        return pl.BlockSpec((tm, w), lambda i: (i, 0))

    consts = [wa, wb, wo, g3, wg, wu, wd, gf]
    return pl.pallas_call(
        _merge_ffn_kernel,
        grid=(n // tm,),
        in_specs=[row(d), row(oa.shape[1]), row(ob.shape[1]), row(d), row(d)]
        + [_const_spec(c.shape) for c in consts],
        out_specs=row(d),
        out_shape=jax.ShapeDtypeStruct((n, d), F32),
        compiler_params=pltpu.CompilerParams(
            dimension_semantics=("parallel",), vmem_limit_bytes=VMEM_LIMIT_BYTES),
        name="merge_ffn",
    )(x, oa, ob, ga, gb, *consts)


def _stack_halves(q):
    lane = lax.broadcasted_iota(jnp.int32, q.shape, 1)
    zero = jnp.zeros_like(q)
    return jnp.concatenate([jnp.where(lane < LANES // 2, q, zero),
                            jnp.where(lane >= LANES // 2, q, zero)], axis=0)


def _diff_prompt_kernel(q_ref, kt_ref, v_ref, lq1_ref, lk1_ref, lq2_ref, lk2_ref, g_ref,
                        o_ref, m_sc, l_sc, acc_sc, *, lam_init):
    qi = pl.program_id(2)
    tq = q_ref.shape[0]
    qq = _stack_halves(q_ref[...])
    m_sc[...] = jnp.full(m_sc.shape, NEG_INF, F32)
    l_sc[...] = jnp.zeros(l_sc.shape, F32)
    acc_sc[...] = jnp.zeros(acc_sc.shape, F32)

    def block(kb, diagonal):
        start = pl.multiple_of(kb * tq, tq)
        kt = kt_ref[:, pl.ds(start, tq)]
        v = v_ref[pl.ds(start, tq), :]
        s = _dot(qq, kt)
        if diagonal:
            row = lax.broadcasted_iota(jnp.int32, s.shape, 0) & (tq - 1)
            col = lax.broadcasted_iota(jnp.int32, s.shape, 1)
            s = jnp.where(col <= row, s, NEG_INF)
        m_prev = m_sc[...]
        m_new = jnp.maximum(m_prev, jnp.max(s, axis=-1, keepdims=True))
        alpha = jnp.exp(m_prev - m_new)
        p = jnp.exp(s - m_new)
        l_sc[...] = alpha * l_sc[...] + jnp.sum(p, axis=-1, keepdims=True)
        acc_sc[...] = alpha * acc_sc[...] + _dot(p.astype(BF16), v)
        m_sc[...] = m_new

    def body(kb, carry):
        block(kb, False)
        return carry

    lax.fori_loop(0, qi, body, 0)
    block(qi, True)

    lam = _lam_from_refs(lq1_ref, lk1_ref, lq2_ref, lk2_ref, lam_init)
    o = acc_sc[...] / l_sc[...]
    o = o[:tq] - lam * o[tq:]
    o_ref[...] = (_rmsnorm(o, g_ref[...]) * (1.0 - lam_init)).astype(o_ref.dtype)


def _diff_prompt(q, kt, v, lams, g, *, lam_init, tq):
    b, s, w = q.shape
    heads = w // LANES
    assert tq & (tq - 1) == 0 and s % tq == 0
    q_spec = pl.BlockSpec((None, tq, LANES), lambda bi, h, qi: (bi, qi, h))
    kt_spec = pl.BlockSpec((None, LANES, s), lambda bi, h, qi: (bi, h, 0))
    v_spec = pl.BlockSpec((None, s, LANES), lambda bi, h, qi: (bi, 0, h))
    small = [_const_spec(a.shape) for a in (*lams, g)]
    return pl.pallas_call(
        functools.partial(_diff_prompt_kernel, lam_init=lam_init),
        grid=(b, heads, s // tq),
        in_specs=[q_spec, kt_spec, v_spec] + small,
        out_specs=q_spec,
        out_shape=jax.ShapeDtypeStruct((b, s, w), BF16),
        scratch_shapes=[pltpu.VMEM((2 * tq, 1), F32), pltpu.VMEM((2 * tq, 1), F32),
                        pltpu.VMEM((2 * tq, LANES), F32)],
        compiler_params=pltpu.CompilerParams(
            dimension_semantics=("parallel", "parallel", "arbitrary"),
            vmem_limit_bytes=VMEM_LIMIT_BYTES),
        name="diff_attn_prompt",
    )(q, kt, v, *lams, g)


def _sb_prompt_kernel(q_ref, kt_ref, vt_ref, o_ref, c_sc, acc_sc):
    qi = pl.program_id(2)
    tq = q_ref.shape[0]
    qq = _stack_halves(q_ref[...])
    u = _suffix_sum_matrix(tq)
    c_sc[...] = jnp.zeros(c_sc.shape, F32)
    acc_sc[...] = jnp.zeros(acc_sc.shape, F32)

    def block(kb, diagonal):
        start = pl.multiple_of(kb * tq, tq)
        kt = kt_ref[:, pl.ds(start, tq)]
        vt = vt_ref[:, pl.ds(start, tq)]
        z = _dot(qq, kt)
        lk = _neg_softplus(z)
        if diagonal:
            row = lax.broadcasted_iota(jnp.int32, z.shape, 0) & (tq - 1)
            col = lax.broadcasted_iota(jnp.int32, z.shape, 1)
            keep = col < row
            lk = jnp.where(keep, lk, 0.0)
        cs = _inclusive_suffix_sum(lk, u)
        a = jnp.exp(z + cs + c_sc[...])
        if diagonal:
            a = jnp.where(keep, a, 0.0)
        acc_sc[...] += _dot_nt(a.astype(BF16), vt)
        c_sc[...] += cs[:, :1]

    block(qi, True)

    def body(i, carry):
        block(qi - 1 - i, False)
        return carry

    lax.fori_loop(0, qi, body, 0)
    acc = acc_sc[...]
    lane = lax.broadcasted_iota(jnp.int32, (tq, LANES), 1)
    o_ref[...] = jnp.where(lane < LANES // 2, acc[:tq], acc[tq:]).astype(o_ref.dtype)


def _sb_prompt(q, kt, vt, *, tq):
    b, s, w = q.shape
    groups = w // LANES
    assert tq & (tq - 1) == 0 and s % tq == 0
    q_spec = pl.BlockSpec((None, tq, LANES), lambda bi, h, qi: (bi, qi, h))
    t_spec = pl.BlockSpec((None, LANES, s), lambda bi, h, qi: (bi, h, 0))
    return pl.pallas_call(
        _sb_prompt_kernel,
        grid=(b, groups, s // tq),
        in_specs=[q_spec, t_spec, t_spec],
        out_specs=q_spec,
        out_shape=jax.ShapeDtypeStruct((b, s, w), BF16),
        scratch_shapes=[pltpu.VMEM((2 * tq, 1), F32), pltpu.VMEM((2 * tq, LANES), F32)],
        compiler_params=pltpu.CompilerParams(
            dimension_semantics=("parallel", "parallel", "arbitrary"),
            vmem_limit_bytes=VMEM_LIMIT_BYTES),
        name="sb_attn_prompt",
    )(q, kt, vt)


def _row_select(q, tok_of_row):
    out = jnp.zeros(tok_of_row.shape, q.dtype)
    for t in range(q.shape[0]):
        out = jnp.where(tok_of_row == t, jnp.broadcast_to(q[t:t + 1, :], out.shape), out)
    return out


def _new_token_mask(tok_of_row, n_tok, seq, strict):
    key = lax.broadcasted_iota(jnp.int32, tok_of_row.shape, 1)
    key_tok = key & (n_tok - 1)
    causal = key_tok < tok_of_row if strict else key_tok <= tok_of_row
    return ((key >> _log2(n_tok)) == seq) & causal


def _diff_sample_kernel(pt_ref, q_ref, knt_ref, vn_ref, lq1_ref, lk1_ref, lq2_ref, lk2_ref,
                        g_ref, *rest, n_pages_step, n_tok, lam_init):
    kt_refs = rest[:n_pages_step]
    v_refs = rest[n_pages_step:2 * n_pages_step]
    o_ref = rest[2 * n_pages_step]
    qbd_sc, m_sc, l_sc, acc_sc = rest[2 * n_pages_step + 1:]
    b = pl.program_id(0)
    j = pl.program_id(1)
    rows, w = qbd_sc.shape
    page = kt_refs[0].shape[1]
    row_head = (lax.broadcasted_iota(jnp.int32, (rows, DA_V_DIM), 0) >> _log2(n_tok)) \
        & (DA_HEADS - 1)

    def attend(kts, v_heads, mask):
        qbd = qbd_sc[...]
        ss = [_dot(qbd, kt.astype(BF16)) for kt in kts]
        if mask is not None:
            ss = [jnp.where(mask, s, NEG_INF) for s in ss]
        m_prev = m_sc[...]
        m_new = m_prev
        for s in ss:
            m_new = jnp.maximum(m_new, jnp.max(s, axis=-1, keepdims=True))
        alpha = jnp.exp(m_prev - m_new)
        l_new = alpha * l_sc[...]
        acc = alpha * acc_sc[...]
        for s, v_head in zip(ss, v_heads):
            p = jnp.exp(s - m_new)
            l_new = l_new + jnp.sum(p, axis=-1, keepdims=True)
            pb = p.astype(BF16)
            for h in range(DA_HEADS):
                pv = _dot(pb, v_head(h).astype(BF16))
                acc = acc + jnp.where(row_head == h, pv, 0.0)
        m_sc[...] = m_new
        l_sc[...] = l_new
        acc_sc[...] = acc

    @pl.when(j == 0)
    def _():
        row = lax.broadcasted_iota(jnp.int32, (rows, w), 0)
        lane = lax.broadcasted_iota(jnp.int32, (rows, w), 1)
        tok = row & (n_tok - 1)
        head_comp = ((row >> _log2(n_tok)) & (DA_HEADS - 1)) * 2 + (row >> _log2(n_tok * DA_HEADS))
        q_rows = _row_select(q_ref[...], tok)
        qbd = jnp.where((lane >> _log2(DA_HEAD_DIM)) == head_comp, q_rows, 0.0)
        qbd_sc[...] = qbd.astype(BF16)
        m_sc[...] = jnp.full(m_sc.shape, NEG_INF, F32)
        l_sc[...] = jnp.zeros(l_sc.shape, F32)
        acc_sc[...] = jnp.zeros(acc_sc.shape, F32)
        n_new = knt_ref.shape[1]
        tok_n = lax.broadcasted_iota(jnp.int32, (rows, n_new), 0) & (n_tok - 1)
        mask = _new_token_mask(tok_n, n_tok, b, False)
        attend([knt_ref[...]],
               [lambda h: vn_ref[:, h * DA_V_DIM:(h + 1) * DA_V_DIM]], mask)

    attend([r[...] for r in kt_refs],
           [lambda h, r=r: r[pl.ds(h, page, stride=DA_HEADS), :] for r in v_refs], None)

    @pl.when(j == pl.num_programs(1) - 1)
    def _():
        lam = _lam_from_refs(lq1_ref, lk1_ref, lq2_ref, lk2_ref, lam_init)
        o = acc_sc[...] / l_sc[...]
        o = o[:rows // 2] - lam * o[rows // 2:]
        o_ref[...] = _rmsnorm(o, g_ref[...]) * (1.0 - lam_init)


def _sb_sample_kernel(pt_ref, q_ref, knt_ref, vnt_ref, *rest, n_pages_step, n_tok):
    kt_refs = rest[:n_pages_step]
    vt_refs = rest[n_pages_step:2 * n_pages_step]
    o_ref = rest[2 * n_pages_step]
    qbd_sc, c_sc, acc_sc = rest[2 * n_pages_step + 1:]
    b = pl.program_id(0)
    j = pl.program_id(1)
    rows, w = qbd_sc.shape
    page = kt_refs[0].shape[1]
    u = _suffix_sum_matrix(page)

    def attend(kts, vts, mask):
        qbd = qbd_sc[...]
        c = c_sc[...]
        acc = acc_sc[...]
        for kt, vt in zip(kts, vts):
            z = _dot(qbd, kt.astype(BF16))
            lk = _neg_softplus(z)
            if mask is not None:
                lk = jnp.where(mask, lk, 0.0)
            cs = _inclusive_suffix_sum(lk, u)
            a = jnp.exp(z + cs + c)
            if mask is not None:
                a = jnp.where(mask, a, 0.0)
            acc = acc + _dot_nt(a.astype(BF16), vt.astype(BF16))
            c = c + cs[:, :1]
        c_sc[...] = c
        acc_sc[...] = acc

    @pl.when(j == 0)
    def _():
        row = lax.broadcasted_iota(jnp.int32, (rows, w), 0)
        lane = lax.broadcasted_iota(jnp.int32, (rows, w), 1)
        q_rows = _row_select(q_ref[...], row >> _log2(SB_HEADS))
        qbd = jnp.where((lane >> _log2(SB_HEAD_DIM)) == (row & (SB_HEADS - 1)), q_rows, 0.0)
        qbd_sc[...] = qbd.astype(BF16)
        c_sc[...] = jnp.zeros(c_sc.shape, F32)
        acc_sc[...] = jnp.zeros(acc_sc.shape, F32)
        n_new = knt_ref.shape[1]
        assert n_new == page
        tok_n = lax.broadcasted_iota(jnp.int32, (rows, n_new), 0) >> _log2(SB_HEADS)
        mask = _new_token_mask(tok_n, n_tok, b, True)
        attend([knt_ref[...]], [vnt_ref[...]], mask)

    attend([r[...] for r in kt_refs], [r[...] for r in vt_refs], None)

    @pl.when(j == pl.num_programs(1) - 1)
    def _():
        acc = acc_sc[...]
        h = lax.broadcasted_iota(jnp.int32, (SB_HEADS, w), 0)
        lane = lax.broadcasted_iota(jnp.int32, (SB_HEADS, w), 1)
        keep = (lane >> _log2(SB_HEAD_DIM)) == h
        for t in range(n_tok):
            tile = acc[t * SB_HEADS:(t + 1) * SB_HEADS]
            o_ref[t:t + 1, :] = jnp.sum(jnp.where(keep, tile, 0.0), axis=0, keepdims=True)


def _page_index(b, j, pt_ref, *, slot, per_step, n_pages, reverse):
    p = j * per_step + slot
    if reverse:
        p = n_pages - 1 - p
    return (pt_ref[b, p], 0, 0)


def _sample_attention(kern, q, new_k, new_v, extras, cache_k, cache_v, page_table, *,
                      reverse, out_block, scratch, name):
    n_seq, n_tok, w = q.shape
    n_pages = page_table.shape[1]
    per_step = PAGES_PER_STEP
    assert n_pages % per_step == 0
    seq_spec = pl.BlockSpec((None, n_tok, w), lambda b, j, pt: (b, 0, 0))
    out_spec = pl.BlockSpec((None, *out_block), lambda b, j, pt: (b, 0, 0))

    def page_specs(cache):
        return [pl.BlockSpec((None, *cache.shape[1:]),
                             functools.partial(_page_index, slot=i, per_step=per_step,
                                               n_pages=n_pages, reverse=reverse))
                for i in range(per_step)]

    grid_spec = pltpu.PrefetchScalarGridSpec(
        num_scalar_prefetch=1,
        grid=(n_seq, n_pages // per_step),
        in_specs=[seq_spec] + [_const_spec(e.shape) for e in (new_k, new_v, *extras)]
        + page_specs(cache_k) + page_specs(cache_v),
        out_specs=out_spec,
        scratch_shapes=scratch,
    )
    return pl.pallas_call(
        functools.partial(kern, n_pages_step=per_step, n_tok=n_tok),
        grid_spec=grid_spec,
        out_shape=jax.ShapeDtypeStruct((n_seq, *out_block), F32),
        compiler_params=pltpu.CompilerParams(
            dimension_semantics=("parallel", "arbitrary"),
            vmem_limit_bytes=VMEM_LIMIT_BYTES),
        name=name,
    )(page_table, q, new_k, new_v, *extras, *([cache_k] * per_step), *([cache_v] * per_step))


def _rope_tables(pos):
    half = ROPE_DIM // 2
    inv_freq = ROPE_THETA ** (-jnp.arange(0, ROPE_DIM, 2, dtype=F32) / ROPE_DIM)
    ang = pos.astype(F32)[:, None] * inv_freq[None, :]
    cos, sin = jnp.cos(ang), jnp.sin(ang)
    n = pos.shape[0]
    rest = DA_HEAD_DIM - ROPE_DIM
    zeros_h = jnp.zeros((n, half), F32)
    cos_r = jnp.concatenate([cos, cos, jnp.ones((n, rest), F32)], axis=1)
    sin_next = jnp.concatenate([-sin, zeros_h, jnp.zeros((n, rest), F32)], axis=1)
    sin_prev = jnp.concatenate([zeros_h, sin, jnp.zeros((n, rest), F32)], axis=1)
    reps = LANES // DA_HEAD_DIM
    return (tuple(jnp.tile(t, (1, reps)) for t in (cos_r, sin_next, sin_prev)),
            (cos.T, sin.T))


def kernel(x_prompt, x_sample, cache_diff_k, cache_diff_v, cache_sb_k, cache_sb_v, page_table,
           norm1, w_ffn1_gate, w_ffn1_up, w_ffn1_down, norm2, w_in,
           lambda_q1, lambda_k1, lambda_q2, lambda_k2, subln_g,
           w_branch_a, w_branch_b, w_out, norm3, w_ffn2_gate, w_ffn2_up, w_ffn2_down, norm_f):
    bsz, seq, d = x_prompt.shape
    n_seq, n_tok, _ = x_sample.shape
    depth = norm1.shape[0]
    assert depth == 1, "the final norm is fused into the last layer's second FFN"
    pool, page = cache_diff_k.shape[1:3]
    past_len = page_table.shape[1] * page
    qk_w = DA_HEADS * 2 * DA_HEAD_DIM
    v_w = DA_HEADS * DA_V_DIM
    sb_w = SB_HEADS * SB_HEAD_DIM
    n_s = n_seq * n_tok

    xp = x_prompt.reshape(bsz * seq, d)
    xs = x_sample.reshape(n_s, d)
    tabs_p = _rope_tables(jnp.arange(seq))
    tabs_s = _rope_tables(jnp.tile(past_len + jnp.arange(n_tok), n_seq))
    gf = norm_f.reshape(1, d)
    new_kv_p, new_kv_s = [], []

    for l in range(depth):
        lam_init = 0.8 - 0.6 * math.exp(-0.3 * l)
        row = lambda a: a[l][None]
        cast = lambda a: a[l].astype(BF16)
        lams = tuple(row(a) for a in (lambda_q1, lambda_k1, lambda_q2, lambda_k2))
        ffn1 = (row(norm1), cast(w_ffn1_gate), cast(w_ffn1_up), cast(w_ffn1_down))
        merge = (cast(w_branch_a), cast(w_branch_b), cast(w_out), row(norm3),
                 cast(w_ffn2_gate), cast(w_ffn2_up), cast(w_ffn2_down), gf)
        w_in_l = cast(w_in)
        w_t = jnp.concatenate(
            [w_in_l[:, o:o + qk_w].T for o in (qk_w, 2 * qk_w + v_w + sb_w,
                                               2 * qk_w + v_w + 2 * sb_w)], axis=0)
        g_sub = row(subln_g)

        xp = _ffn_half(xp, *ffn1, tm=FFN_TOKEN_TILE)
        (qa, qb, ka_t, va, kb_t, vb_t, ga, gb, ka_h, va_h, kb_h, vb_h) = _proj(
            xp, row(norm2), w_in_l, w_t, *tabs_p, tm=PROJ_TOKEN_TILE, sample=False)
        shp = lambda a: a.reshape(bsz, seq, a.shape[-1])
        oa = _diff_prompt(shp(qa), ka_h, shp(va_h), lams, g_sub, lam_init=lam_init,
                          tq=ATTN_BLOCK)
        ob = _sb_prompt(shp(qb), kb_h, vb_h, tq=ATTN_BLOCK)
        xp = _merge_ffn(xp, oa.reshape(bsz * seq, v_w), ob.reshape(bsz * seq, sb_w), ga, gb,
                        *merge, tm=FFN_TOKEN_TILE)
        new_kv_p.append((
            jnp.transpose(ka_t.reshape(bsz, DA_HEADS, 2, DA_HEAD_DIM, seq), (0, 4, 1, 2, 3)),
            va.reshape(bsz, seq, DA_HEADS, DA_V_DIM),
            jnp.transpose(kb_t.reshape(bsz, SB_HEADS, SB_HEAD_DIM, seq), (0, 3, 1, 2)),
            jnp.transpose(vb_t.reshape(bsz, SB_HEADS, SB_HEAD_DIM, seq), (0, 3, 1, 2))))

        xs = _ffn_half(xs, *ffn1, tm=n_s)
        (qa, qb, ka_t, va, kb_t, vb_t, ga, gb, ka, kb, vb) = _proj(
            xs, row(norm2), w_in_l, w_t, *tabs_s, tm=n_s, sample=True)
        shs = lambda a: a.reshape(n_seq, n_tok, a.shape[-1])
        rows = 2 * DA_HEADS * n_tok
        ck_a = jnp.transpose(cache_diff_k[l], (0, 2, 3, 4, 1)).reshape(pool, qk_w, page)
        cv_a = cache_diff_v[l].reshape(pool, page * DA_HEADS, DA_V_DIM)
        ck_b = jnp.transpose(cache_sb_k[l], (0, 2, 3, 1)).reshape(pool, sb_w, page)
        cv_b = jnp.transpose(cache_sb_v[l], (0, 2, 3, 1)).reshape(pool, sb_w, page)
        oa = _sample_attention(
            functools.partial(_diff_sample_kernel, lam_init=lam_init),
            shs(qa), ka_t[0], va, (*lams, g_sub), ck_a, cv_a, page_table,
            reverse=False, out_block=(DA_HEADS * n_tok, DA_V_DIM),
            scratch=[pltpu.VMEM((rows, qk_w), BF16), pltpu.VMEM((rows, 1), F32),
                     pltpu.VMEM((rows, 1), F32), pltpu.VMEM((rows, DA_V_DIM), F32)],
            name="diff_attn_sample")
        oa = jnp.transpose(oa.reshape(n_seq, DA_HEADS, n_tok, DA_V_DIM), (0, 2, 1, 3))
        rows = SB_HEADS * n_tok
        ob = _sample_attention(
            _sb_sample_kernel, shs(qb), kb_t[0], vb_t[0], (), ck_b, cv_b, page_table,
            reverse=True, out_block=(n_tok, sb_w),
            scratch=[pltpu.VMEM((rows, sb_w), BF16), pltpu.VMEM((rows, 1), F32),
                     pltpu.VMEM((rows, sb_w), F32)],
            name="sb_attn_sample")
        xs = _merge_ffn(xs, oa.reshape(n_s, v_w), ob.reshape(n_s, sb_w), ga, gb, *merge, tm=n_s)
        new_kv_s.append((ka.reshape(n_seq, n_tok, DA_HEADS, 2, DA_HEAD_DIM),
                         va.reshape(n_seq, n_tok, DA_HEADS, DA_V_DIM),
                         kb.reshape(n_seq, n_tok, SB_HEADS, SB_HEAD_DIM),
                         vb.reshape(n_seq, n_tok, SB_HEADS, SB_HEAD_DIM)))

    stack = lambda items, i: jnp.stack([it[i] for it in items])
    return (xp.reshape(bsz, seq, d), xs.reshape(n_seq, n_tok, d),
            *(stack(new_kv_p, i) for i in range(4)),
            *(stack(new_kv_s, i) for i in range(4)))
```

```python
import functools
import math

import jax
import jax.numpy as jnp
from jax import lax
from jax.experimental import pallas as pl
from jax.experimental.pallas import tpu as pltpu

F32 = jnp.float32
BF16 = jnp.bfloat16

DA_HEADS = 4
DA_HEAD_DIM = 64
DA_V_DIM = 2 * DA_HEAD_DIM
SB_HEADS = 8
SB_HEAD_DIM = 64
ROPE_THETA = 500000.0
ROPE_DIM = DA_HEAD_DIM // 4
FFN_RES = 0.5
RMS_EPS = 1e-6
NEG_INF = -1e30

LANES = 128
SUBLANES = 8
VMEM_LIMIT_BYTES = 56 * 1024 * 1024

FFN_TOKEN_TILE = 512
FFN_CHUNK = 1408
PROJ_TOKEN_TILE = 256
ATTN_QUERY_TILE = 512
ATTN_KEY_BLOCK = 256
PAGES_PER_STEP = 8


def _const_spec(shape):
    zeros = (0,) * len(shape)
    return pl.BlockSpec(shape, lambda *_: zeros, pipeline_mode=pl.Buffered(1))


def _log2(n):
    assert n & (n - 1) == 0
    return n.bit_length() - 1


def _rmsnorm(x, g):
    ms = jnp.mean(x * x, axis=-1, keepdims=True)
    return x * lax.rsqrt(ms + RMS_EPS) * g


def _dot(a, b):
    return jnp.dot(a, b, preferred_element_type=F32)


def _dot_nt(a, b):
    return lax.dot_general(a, b, (((1,), (1,)), ((), ())), preferred_element_type=F32)


def _swiglu(xn, wg_ref, wu_ref, wd_ref):
    xb = xn.astype(BF16)
    d_ff = wg_ref.shape[1]
    chunk = FFN_CHUNK if d_ff % FFN_CHUNK == 0 else d_ff
    acc = None
    for c in range(d_ff // chunk):
        lo, hi = c * chunk, (c + 1) * chunk
        hg = _dot(xb, wg_ref[:, lo:hi])
        hu = _dot(xb, wu_ref[:, lo:hi])
        h = (hg * jax.nn.sigmoid(hg) * hu).astype(BF16)
        part = _dot(h, wd_ref[lo:hi, :])
        acc = part if acc is None else acc + part
    return acc


def _lam_from_refs(lq1_ref, lk1_ref, lq2_ref, lk2_ref, lam_init):
    a1 = jnp.sum(lq1_ref[...] * lk1_ref[...], axis=-1, keepdims=True)
    a2 = jnp.sum(lq2_ref[...] * lk2_ref[...], axis=-1, keepdims=True)
    return jnp.exp(a1) - jnp.exp(a2) + lam_init


LOG2_E = 1.4426950408889634


def _log2_keep(z):
    z2 = z * LOG2_E
    nz2 = z * (-LOG2_E)
    return jnp.minimum(nz2, 0.0) - jnp.log2(1.0 + jnp.exp2(jnp.minimum(z2, nz2))), z2


def _suffix_sum_matrix(n):
    j = lax.broadcasted_iota(jnp.int32, (2 * n, n), 0) & (n - 1)
    k = lax.broadcasted_iota(jnp.int32, (2 * n, n), 1)
    return jnp.where(j >= k, 1.0, 0.0).astype(BF16)


def _inclusive_suffix_sum(x, u2):
    hi = x.astype(BF16)
    lo = (x - hi.astype(F32)).astype(BF16)
    return _dot(jnp.concatenate([hi, lo], axis=1), u2)


def _ffn_half_kernel(x_ref, g_ref, wg_ref, wu_ref, wd_ref, o_ref):
    x = x_ref[...]
    o_ref[...] = x + FFN_RES * _swiglu(_rmsnorm(x, g_ref[...]), wg_ref, wu_ref, wd_ref)


def _ffn_half(x, g, wg, wu, wd, *, tm):
    n, d = x.shape
    d_ff = wg.shape[1]
    row = pl.BlockSpec((tm, d), lambda i: (i, 0))
    return pl.pallas_call(
        _ffn_half_kernel,
        grid=(n // tm,),
        in_specs=[row, _const_spec((1, d)), _const_spec((d, d_ff)), _const_spec((d, d_ff)),
                  _const_spec((d_ff, d))],
        out_specs=row,
        out_shape=jax.ShapeDtypeStruct((n, d), F32),
        compiler_params=pltpu.CompilerParams(
            dimension_semantics=("parallel",), vmem_limit_bytes=VMEM_LIMIT_BYTES),
        name="ffn_half",
    )(x, g, wg, wu, wd)


def _rope_rows(t, cos, sin_next, sin_prev):
    half = ROPE_DIM // 2
    groups = []
    for c in range(0, t.shape[1], LANES):
        x = t[:, c:c + LANES]
        nxt = pltpu.roll(x, LANES - half, axis=1)
        prv = pltpu.roll(x, half, axis=1)
        groups.append(x * cos + nxt * sin_next + prv * sin_prev)
    return jnp.concatenate(groups, axis=1)


def _rope_cols(t, cos_t, sin_t):
    half = ROPE_DIM // 2
    assert half == SUBLANES
    pieces = []
    for r in range(0, t.shape[0], DA_HEAD_DIM):
        x1 = t[r:r + half]
        x2 = t[r + half:r + 2 * half]
        pieces += [x1 * cos_t - x2 * sin_t, x2 * cos_t + x1 * sin_t,
                   t[r + 2 * half:r + DA_HEAD_DIM]]
    return jnp.concatenate(pieces, axis=0)


_QA, _KA, _VA, _QB, _KB, _VB, _GA, _GB = range(8)


def _proj_kernel(x_ref, g_ref, w_ref, wt_ref, cos_ref, sn_ref, sp_ref, cos_t_ref, sin_t_ref,
                 *outs, offs, da_scale, sb_scale, sample):
    xb = _rmsnorm(x_ref[...], g_ref[...]).astype(BF16)

    def seg(i):
        return _dot(xb, w_ref[:, offs[i]:offs[i + 1]])

    def seg_t(i):
        w = offs[_KA + 1] - offs[_KA]
        return _dot_nt(wt_ref[i * w:(i + 1) * w, :], xb)

    qa_o, qb_o, ka_t, va_o, kb_t, vb_t, ga_o, gb_o = outs[:8]
    rope_tabs = (cos_ref[...], sn_ref[...], sp_ref[...])
    qa_o[...] = (_rope_rows(seg(_QA), *rope_tabs) * da_scale).astype(qa_o.dtype)
    qb_o[...] = (seg(_QB) * sb_scale).astype(qb_o.dtype)
    ka = _rope_cols(seg_t(0), cos_t_ref[...], sin_t_ref[...])
    kb = seg_t(1)
    vb = seg_t(2)
    va = seg(_VA)
    ka_t[...] = ka
    va_o[...] = va
    kb_t[...] = kb
    vb_t[...] = vb
    ga_o[...] = seg(_GA)
    gb_o[...] = seg(_GB)
    if sample:
        ka_o, kb_o, vb_o = outs[8:]
        ka_o[...] = _rope_rows(seg(_KA), *rope_tabs)
        kb_o[...] = seg(_KB)
        vb_o[...] = seg(_VB)
    else:
        ka_h, va_h, kb_h, vb_h = outs[8:]
        ka_h[...] = ka.astype(BF16)
        va_h[...] = va.astype(BF16)
        kb_h[...] = kb.astype(BF16)
        vb_h[...] = vb.astype(BF16)


def _proj(x, g, w_in, w_t, tabs_rows, tabs_cols, *, tm, sample):
    n, d = x.shape
    qk_w = DA_HEADS * 2 * DA_HEAD_DIM
    v_w = DA_HEADS * DA_V_DIM
    sb_w = SB_HEADS * SB_HEAD_DIM
    assert qk_w == sb_w and w_t.shape[0] == 3 * qk_w
    widths = (qk_w, qk_w, v_w, sb_w, sb_w, sb_w, d, d)
    assert sum(widths) == w_in.shape[1]
    offs = tuple(sum(widths[:i]) for i in range(len(widths) + 1))
    period = tabs_rows[0].shape[0]
    pos_blocks = period // tm
    n_batch = n // period
    q_dtype = F32 if sample else BF16

    def row(w):
        return pl.BlockSpec((tm, w), lambda i: (i, 0))

    col = pl.BlockSpec((None, qk_w, tm), lambda i: (i // pos_blocks, 0, i % pos_blocks))
    tab_r = pl.BlockSpec((tm, LANES), lambda i: (i % pos_blocks, 0))
    tab_c = pl.BlockSpec((ROPE_DIM // 2, tm), lambda i: (0, i % pos_blocks))

    def rows(w, dt):
        return jax.ShapeDtypeStruct((n, w), dt), row(w)

    def cols(dt):
        return jax.ShapeDtypeStruct((n_batch, qk_w, period), dt), col

    outs = [rows(qk_w, q_dtype), rows(sb_w, q_dtype), cols(F32), rows(v_w, F32), cols(F32),
            cols(F32), rows(d, F32), rows(d, F32)]
    if sample:
        outs += [rows(qk_w, F32), rows(sb_w, F32), rows(sb_w, F32)]
    else:
        outs += [cols(BF16), rows(v_w, BF16), cols(BF16), cols(BF16)]
    kern = functools.partial(_proj_kernel, offs=offs, da_scale=DA_HEAD_DIM ** -0.5,
                             sb_scale=SB_HEAD_DIM ** -0.5, sample=sample)
    return pl.pallas_call(
        kern,
        grid=(n // tm,),
        in_specs=[row(d), _const_spec((1, d)), _const_spec(w_in.shape), _const_spec(w_t.shape),
                  tab_r, tab_r, tab_r, tab_c, tab_c],
        out_specs=[o[1] for o in outs],
        out_shape=[o[0] for o in outs],
        compiler_params=pltpu.CompilerParams(
            dimension_semantics=("parallel",), vmem_limit_bytes=VMEM_LIMIT_BYTES),
        name="in_proj",
    )(x, g, w_in, w_t, *tabs_rows, *tabs_cols)


def _merge_ffn_kernel(x_ref, oa_ref, ob_ref, ga_ref, gb_ref, wa_ref, wb_ref, wo_ref,
                      g3_ref, wg_ref, wu_ref, wd_ref, gf_ref, o_ref):
    ya = _dot(oa_ref[...].astype(BF16), wa_ref[...])
    yb = _dot(ob_ref[...].astype(BF16), wb_ref[...])
    m = jax.nn.sigmoid(ga_ref[...]) * ya + jax.nn.sigmoid(gb_ref[...]) * yb
    x = x_ref[...] + _dot(m.astype(BF16), wo_ref[...])
    y = x + FFN_RES * _swiglu(_rmsnorm(x, g3_ref[...]), wg_ref, wu_ref, wd_ref)
    o_ref[...] = _rmsnorm(y, gf_ref[...])


def _merge_ffn(x, oa, ob, ga, gb, wa, wb, wo, g3, wg, wu, wd, gf, *, tm):
    n, d = x.shape

    def row(w):
        return pl.BlockSpec((tm, w), lambda i: (i, 0))

    consts = [wa, wb, wo, g3, wg, wu, wd, gf]
    return pl.pallas_call(
        _merge_ffn_kernel,
        grid=(n // tm,),
        in_specs=[row(d), row(oa.shape[1]), row(ob.shape[1]), row(d), row(d)]
        + [_const_spec(c.shape) for c in consts],
        out_specs=row(d),
        out_shape=jax.ShapeDtypeStruct((n, d), F32),
        compiler_params=pltpu.CompilerParams(
            dimension_semantics=("parallel",), vmem_limit_bytes=VMEM_LIMIT_BYTES),
        name="merge_ffn",
    )(x, oa, ob, ga, gb, *consts)


def _stack_halves(q):
    lane = lax.broadcasted_iota(jnp.int32, q.shape, 1)
    zero = jnp.zeros_like(q)
    return jnp.concatenate([jnp.where(lane < LANES // 2, q, zero),
                            jnp.where(lane >= LANES // 2, q, zero)], axis=0)


def _causal_keep(shape, tq, key_offset, strict):
    row = lax.broadcasted_iota(jnp.int32, shape, 0) & (tq - 1)
    key = lax.broadcasted_iota(jnp.int32, shape, 1) + key_offset
    return key < row if strict else key <= row


def _lane_chunks(x):
    return [x[:, c:c + LANES] for c in range(0, x.shape[1], LANES)]


def _diff_prompt_kernel(q_ref, kt_ref, v_ref, lq1_ref, lk1_ref, lq2_ref, lk2_ref, g_ref,
                        o_ref, mx_sc, l_sc, acc_sc, *, lam_init, tk):
    qi = pl.program_id(2)
    tq = q_ref.shape[0]
    diag_blocks = tq // tk
    n_full = qi * diag_blocks
    qq = _stack_halves(q_ref[...])

    def scores(kb):
        start = pl.multiple_of(kb * tk, tk)
        return _dot(qq, kt_ref[:, pl.ds(start, tk)])

    def diag_scores(d):
        s = scores(n_full + d)
        return jnp.where(_causal_keep(s.shape, tq, d * tk, False), s, NEG_INF)

    mx_sc[...] = jnp.full(mx_sc.shape, NEG_INF, F32)

    def fold_max(s):
        mx = mx_sc[...]
        for chunk in _lane_chunks(s):
            mx = jnp.maximum(mx, chunk)
        mx_sc[...] = mx

    def max_body(kb, carry):
        fold_max(scores(kb))
        return carry

    lax.fori_loop(0, n_full, max_body, 0)
    for d in range(diag_blocks):
        fold_max(diag_scores(d))
    m = jnp.max(mx_sc[...], axis=-1, keepdims=True)
    mx_sc[...] = jnp.broadcast_to(m, mx_sc.shape)

    l_sc[...] = jnp.zeros(l_sc.shape, F32)
    acc_sc[...] = jnp.zeros(acc_sc.shape, F32)

    def accumulate(kb, s):
        start = pl.multiple_of(kb * tk, tk)
        m_rep = mx_sc[...]
        ps = [jnp.exp(chunk - m_rep) for chunk in _lane_chunks(s)]
        l_new = l_sc[...]
        for p in ps:
            l_new = l_new + p
        l_sc[...] = l_new
        p = jnp.concatenate(ps, axis=1).astype(BF16)
        acc_sc[...] += _dot(p, v_ref[pl.ds(start, tk), :])

    def acc_body(kb, carry):
        accumulate(kb, scores(kb))
        return carry

    lax.fori_loop(0, n_full, acc_body, 0)
    for d in range(diag_blocks):
        accumulate(n_full + d, diag_scores(d))

    lam = _lam_from_refs(lq1_ref, lk1_ref, lq2_ref, lk2_ref, lam_init)
    o = acc_sc[...] / jnp.sum(l_sc[...], axis=-1, keepdims=True)
    o = o[:tq] - lam * o[tq:]
    o_ref[...] = (_rmsnorm(o, g_ref[...]) * (1.0 - lam_init)).astype(o_ref.dtype)


def _diff_prompt(q, kt, v, lams, g, *, lam_init, tq, tk):
    b, s, w = q.shape
    heads = w // LANES
    assert tq & (tq - 1) == 0 and s % tq == 0 and tq % tk == 0 and tk % LANES == 0
    q_spec = pl.BlockSpec((None, tq, LANES), lambda bi, h, qi: (bi, qi, h))
    kt_spec = pl.BlockSpec((None, LANES, s), lambda bi, h, qi: (bi, h, 0))
    v_spec = pl.BlockSpec((None, s, LANES), lambda bi, h, qi: (bi, 0, h))
    small = [_const_spec(a.shape) for a in (*lams, g)]
    return pl.pallas_call(
        functools.partial(_diff_prompt_kernel, lam_init=lam_init, tk=tk),
        grid=(b, heads, s // tq),
        in_specs=[q_spec, kt_spec, v_spec] + small,
        out_specs=q_spec,
        out_shape=jax.ShapeDtypeStruct((b, s, w), BF16),
        scratch_shapes=[pltpu.VMEM((2 * tq, LANES), F32)] * 3,
        compiler_params=pltpu.CompilerParams(
            dimension_semantics=("parallel", "parallel", "arbitrary"),
            vmem_limit_bytes=VMEM_LIMIT_BYTES),
        name="diff_attn_prompt",
    )(q, kt, v, *lams, g)


def _sb_prompt_kernel(q_ref, kt_ref, vt_ref, o_ref, c_sc, acc_sc, *, tk):
    qi = pl.program_id(2)
    tq = q_ref.shape[0]
    diag_blocks = tq // tk
    n_full = qi * diag_blocks
    qq = _stack_halves(q_ref[...])
    u = _suffix_sum_matrix(tk)
    c_sc[...] = jnp.zeros(c_sc.shape, F32)
    acc_sc[...] = jnp.zeros(acc_sc.shape, F32)

    def block(kb, keep):
        start = pl.multiple_of(kb * tk, tk)
        z = _dot(qq, kt_ref[:, pl.ds(start, tk)])
        lk, z2 = _log2_keep(z)
        if keep is not None:
            lk = jnp.where(keep, lk, 0.0)
        cs = _inclusive_suffix_sum(lk, u)
        a = jnp.exp2(z2 + cs + c_sc[...])
        if keep is not None:
            a = jnp.where(keep, a, 0.0)
        acc_sc[...] += _dot_nt(a.astype(BF16), vt_ref[:, pl.ds(start, tk)])
        c_sc[...] += cs[:, :1]

    for d in reversed(range(diag_blocks)):
        block(n_full + d, _causal_keep((2 * tq, tk), tq, d * tk, True))

    def body(i, carry):
        block(n_full - 1 - i, None)
        return carry

    lax.fori_loop(0, n_full, body, 0)
    acc = acc_sc[...]
    lane = lax.broadcasted_iota(jnp.int32, (tq, LANES), 1)
    o_ref[...] = jnp.where(lane < LANES // 2, acc[:tq], acc[tq:]).astype(o_ref.dtype)


def _sb_prompt(q, kt, vt, *, tq, tk):
    b, s, w = q.shape
    groups = w // LANES
    assert tq & (tq - 1) == 0 and s % tq == 0 and tq % tk == 0 and tk % LANES == 0
    q_spec = pl.BlockSpec((None, tq, LANES), lambda bi, h, qi: (bi, qi, h))
    t_spec = pl.BlockSpec((None, LANES, s), lambda bi, h, qi: (bi, h, 0))
    return pl.pallas_call(
        functools.partial(_sb_prompt_kernel, tk=tk),
        grid=(b, groups, s // tq),
        in_specs=[q_spec, t_spec, t_spec],
        out_specs=q_spec,
        out_shape=jax.ShapeDtypeStruct((b, s, w), BF16),
        scratch_shapes=[pltpu.VMEM((2 * tq, 1), F32), pltpu.VMEM((2 * tq, LANES), F32)],
        compiler_params=pltpu.CompilerParams(
            dimension_semantics=("parallel", "parallel", "arbitrary"),
            vmem_limit_bytes=VMEM_LIMIT_BYTES),
        name="sb_attn_prompt",
    )(q, kt, vt)


def _row_select(q, tok_of_row):
    out = jnp.zeros(tok_of_row.shape, q.dtype)
    for t in range(q.shape[0]):
        out = jnp.where(tok_of_row == t, jnp.broadcast_to(q[t:t + 1, :], out.shape), out)
    return out


def _new_token_mask(tok_of_row, n_tok, seq, strict):
    key = lax.broadcasted_iota(jnp.int32, tok_of_row.shape, 1)
    key_tok = key & (n_tok - 1)
    causal = key_tok < tok_of_row if strict else key_tok <= tok_of_row
    return ((key >> _log2(n_tok)) == seq) & causal


def _diff_sample_kernel(pt_ref, q_ref, knt_ref, vn_ref, lq1_ref, lk1_ref, lq2_ref, lk2_ref,
                        g_ref, *rest, n_pages_step, n_tok, lam_init):
    kt_refs = rest[:n_pages_step]
    v_refs = rest[n_pages_step:2 * n_pages_step]
    o_ref = rest[2 * n_pages_step]
    qbd_sc, m_sc, l_sc, acc_sc = rest[2 * n_pages_step + 1:]
    b = pl.program_id(0)
    j = pl.program_id(1)
    rows, w = qbd_sc.shape
    page = kt_refs[0].shape[1]
    row_head = (lax.broadcasted_iota(jnp.int32, (rows, DA_V_DIM), 0) >> _log2(n_tok)) \
        & (DA_HEADS - 1)

    def attend(kts, v_heads, mask):
        qbd = qbd_sc[...]
        ss = [_dot(qbd, kt.astype(BF16)) for kt in kts]
        if mask is not None:
            ss = [jnp.where(mask, s, NEG_INF) for s in ss]
        m_prev = m_sc[...]
        m_new = m_prev
        for s in ss:
            m_new = jnp.maximum(m_new, jnp.max(s, axis=-1, keepdims=True))
        alpha = jnp.exp(m_prev - m_new)
        l_new = alpha * l_sc[...]
        acc = alpha * acc_sc[...]
        for s, v_head in zip(ss, v_heads):
            p = jnp.exp(s - m_new)
            l_new = l_new + jnp.sum(p, axis=-1, keepdims=True)
            pb = p.astype(BF16)
            for h in range(DA_HEADS):
                pv = _dot(pb, v_head(h).astype(BF16))
                acc = acc + jnp.where(row_head == h, pv, 0.0)
        m_sc[...] = m_new
        l_sc[...] = l_new
        acc_sc[...] = acc

    @pl.when(j == 0)
    def _():
        row = lax.broadcasted_iota(jnp.int32, (rows, w), 0)
        lane = lax.broadcasted_iota(jnp.int32, (rows, w), 1)
        tok = row & (n_tok - 1)
        head_comp = ((row >> _log2(n_tok)) & (DA_HEADS - 1)) * 2 + (row >> _log2(n_tok * DA_HEADS))
        q_rows = _row_select(q_ref[...], tok)
        qbd = jnp.where((lane >> _log2(DA_HEAD_DIM)) == head_comp, q_rows, 0.0)
        qbd_sc[...] = qbd.astype(BF16)
        m_sc[...] = jnp.full(m_sc.shape, NEG_INF, F32)
        l_sc[...] = jnp.zeros(l_sc.shape, F32)
        acc_sc[...] = jnp.zeros(acc_sc.shape, F32)
        n_new = knt_ref.shape[1]
        tok_n = lax.broadcasted_iota(jnp.int32, (rows, n_new), 0) & (n_tok - 1)
        mask = _new_token_mask(tok_n, n_tok, b, False)
        attend([knt_ref[...]],
               [lambda h: vn_ref[:, h * DA_V_DIM:(h + 1) * DA_V_DIM]], mask)

    attend([r[...] for r in kt_refs],
           [lambda h, r=r: r[pl.ds(h, page, stride=DA_HEADS), :] for r in v_refs], None)

    @pl.when(j == pl.num_programs(1) - 1)
    def _():
        lam = _lam_from_refs(lq1_ref, lk1_ref, lq2_ref, lk2_ref, lam_init)
        o = acc_sc[...] / l_sc[...]
        o = o[:rows // 2] - lam * o[rows // 2:]
        o_ref[...] = _rmsnorm(o, g_ref[...]) * (1.0 - lam_init)


def _sb_sample_kernel(pt_ref, q_ref, knt_ref, vnt_ref, *rest, n_pages_step, n_tok):
    kt_refs = rest[:n_pages_step]
    vt_refs = rest[n_pages_step:2 * n_pages_step]
    o_ref = rest[2 * n_pages_step]
    qbd_sc, c_sc, acc_sc = rest[2 * n_pages_step + 1:]
    b = pl.program_id(0)
    j = pl.program_id(1)
    rows, w = qbd_sc.shape
    page = kt_refs[0].shape[1]
    u = _suffix_sum_matrix(page)

    def attend(kts, vts, mask):
        qbd = qbd_sc[...]
        pairs = [_log2_keep(_dot(qbd, kt.astype(BF16))) for kt in kts]
        lks = [lk for lk, _ in pairs]
        if mask is not None:
            lks = [jnp.where(mask, lk, 0.0) for lk in lks]
        cs_all = _inclusive_suffix_sum(jnp.concatenate(lks, axis=0), u)
        c = c_sc[...]
        acc = acc_sc[...]
        for i, ((_, z2), vt) in enumerate(zip(pairs, vts)):
            cs = cs_all[i * rows:(i + 1) * rows]
            a = jnp.exp2(z2 + cs + c)
            if mask is not None:
                a = jnp.where(mask, a, 0.0)
            acc = acc + _dot_nt(a.astype(BF16), vt.astype(BF16))
            c = c + cs[:, :1]
        c_sc[...] = c
        acc_sc[...] = acc

    @pl.when(j == 0)
    def _():
        row = lax.broadcasted_iota(jnp.int32, (rows, w), 0)
        lane = lax.broadcasted_iota(jnp.int32, (rows, w), 1)
        q_rows = _row_select(q_ref[...], row >> _log2(SB_HEADS))
        qbd = jnp.where((lane >> _log2(SB_HEAD_DIM)) == (row & (SB_HEADS - 1)), q_rows, 0.0)
        qbd_sc[...] = qbd.astype(BF16)
        c_sc[...] = jnp.zeros(c_sc.shape, F32)
        acc_sc[...] = jnp.zeros(acc_sc.shape, F32)
        n_new = knt_ref.shape[1]
        assert n_new == page
        tok_n = lax.broadcasted_iota(jnp.int32, (rows, n_new), 0) >> _log2(SB_HEADS)
        mask = _new_token_mask(tok_n, n_tok, b, True)
        attend([knt_ref[...]], [vnt_ref[...]], mask)

    attend([r[...] for r in kt_refs], [r[...] for r in vt_refs], None)

    @pl.when(j == pl.num_programs(1) - 1)
    def _():
        acc = acc_sc[...]
        h = lax.broadcasted_iota(jnp.int32, (SB_HEADS, w), 0)
        lane = lax.broadcasted_iota(jnp.int32, (SB_HEADS, w), 1)
        keep = (lane >> _log2(SB_HEAD_DIM)) == h
        for t in range(n_tok):
            tile = acc[t * SB_HEADS:(t + 1) * SB_HEADS]
            o_ref[t:t + 1, :] = jnp.sum(jnp.where(keep, tile, 0.0), axis=0, keepdims=True)


def _page_index(b, j, pt_ref, *, slot, per_step, n_pages, reverse):
    p = j * per_step + slot
    if reverse:
        p = n_pages - 1 - p
    return (pt_ref[b, p], 0, 0)


def _sample_attention(kern, q, new_k, new_v, extras, cache_k, cache_v, page_table, *,
                      reverse, out_block, scratch, name):
    n_seq, n_tok, w = q.shape
    n_pages = page_table.shape[1]
    per_step = PAGES_PER_STEP
    assert n_pages % per_step == 0
    seq_spec = pl.BlockSpec((None, n_tok, w), lambda b, j, pt: (b, 0, 0))
    out_spec = pl.BlockSpec((None, *out_block), lambda b, j, pt: (b, 0, 0))

    def page_specs(cache):
        return [pl.BlockSpec((None, *cache.shape[1:]),
                             functools.partial(_page_index, slot=i, per_step=per_step,
                                               n_pages=n_pages, reverse=reverse))
                for i in range(per_step)]

    grid_spec = pltpu.PrefetchScalarGridSpec(
        num_scalar_prefetch=1,
        grid=(n_seq, n_pages // per_step),
        in_specs=[seq_spec] + [_const_spec(e.shape) for e in (new_k, new_v, *extras)]
        + page_specs(cache_k) + page_specs(cache_v),
        out_specs=out_spec,
        scratch_shapes=scratch,
    )
    return pl.pallas_call(
        functools.partial(kern, n_pages_step=per_step, n_tok=n_tok),
        grid_spec=grid_spec,
        out_shape=jax.ShapeDtypeStruct((n_seq, *out_block), F32),
        compiler_params=pltpu.CompilerParams(
            dimension_semantics=("parallel", "arbitrary"),
            vmem_limit_bytes=VMEM_LIMIT_BYTES),
        name=name,
    )(page_table, q, new_k, new_v, *extras, *([cache_k] * per_step), *([cache_v] * per_step))


def _rope_tables(pos):
    half = ROPE_DIM // 2
    inv_freq = ROPE_THETA ** (-jnp.arange(0, ROPE_DIM, 2, dtype=F32) / ROPE_DIM)
    ang = pos.astype(F32)[:, None] * inv_freq[None, :]
    cos, sin = jnp.cos(ang), jnp.sin(ang)
    n = pos.shape[0]
    rest = DA_HEAD_DIM - ROPE_DIM
    zeros_h = jnp.zeros((n, half), F32)
    cos_r = jnp.concatenate([cos, cos, jnp.ones((n, rest), F32)], axis=1)
    sin_next = jnp.concatenate([-sin, zeros_h, jnp.zeros((n, rest), F32)], axis=1)
    sin_prev = jnp.concatenate([zeros_h, sin, jnp.zeros((n, rest), F32)], axis=1)
    reps = LANES // DA_HEAD_DIM
    return (tuple(jnp.tile(t, (1, reps)) for t in (cos_r, sin_next, sin_prev)),
            (cos.T, sin.T))


def kernel(x_prompt, x_sample, cache_diff_k, cache_diff_v, cache_sb_k, cache_sb_v, page_table,
           norm1, w_ffn1_gate, w_ffn1_up, w_ffn1_down, norm2, w_in,
           lambda_q1, lambda_k1, lambda_q2, lambda_k2, subln_g,
           w_branch_a, w_branch_b, w_out, norm3, w_ffn2_gate, w_ffn2_up, w_ffn2_down, norm_f):
    bsz, seq, d = x_prompt.shape
    n_seq, n_tok, _ = x_sample.shape
    depth = norm1.shape[0]
    assert depth == 1, "the final norm is fused into the last layer's second FFN"
    pool, page = cache_diff_k.shape[1:3]
    past_len = page_table.shape[1] * page
    qk_w = DA_HEADS * 2 * DA_HEAD_DIM
    v_w = DA_HEADS * DA_V_DIM
    sb_w = SB_HEADS * SB_HEAD_DIM
    n_s = n_seq * n_tok

    xp = x_prompt.reshape(bsz * seq, d)
    xs = x_sample.reshape(n_s, d)
    tabs_p = _rope_tables(jnp.arange(seq))
    tabs_s = _rope_tables(jnp.tile(past_len + jnp.arange(n_tok), n_seq))
    gf = norm_f.reshape(1, d)
    new_kv_p, new_kv_s = [], []

    for l in range(depth):
        lam_init = 0.8 - 0.6 * math.exp(-0.3 * l)
        row = lambda a: a[l][None]
        cast = lambda a: a[l].astype(BF16)
        lams = tuple(row(a) for a in (lambda_q1, lambda_k1, lambda_q2, lambda_k2))
        ffn1 = (row(norm1), cast(w_ffn1_gate), cast(w_ffn1_up), cast(w_ffn1_down))
        merge = (cast(w_branch_a), cast(w_branch_b), cast(w_out), row(norm3),
                 cast(w_ffn2_gate), cast(w_ffn2_up), cast(w_ffn2_down), gf)
        w_in_l = cast(w_in)
        w_t = jnp.concatenate(
            [w_in_l[:, o:o + qk_w].T for o in (qk_w, 2 * qk_w + v_w + sb_w,
                                               2 * qk_w + v_w + 2 * sb_w)], axis=0)
        g_sub = row(subln_g)

        xp = _ffn_half(xp, *ffn1, tm=FFN_TOKEN_TILE)
        (qa, qb, ka_t, va, kb_t, vb_t, ga, gb, ka_h, va_h, kb_h, vb_h) = _proj(
            xp, row(norm2), w_in_l, w_t, *tabs_p, tm=PROJ_TOKEN_TILE, sample=False)
        shp = lambda a: a.reshape(bsz, seq, a.shape[-1])
        oa = _diff_prompt(shp(qa), ka_h, shp(va_h), lams, g_sub, lam_init=lam_init,
                          tq=ATTN_QUERY_TILE, tk=ATTN_KEY_BLOCK)
        ob = _sb_prompt(shp(qb), kb_h, vb_h, tq=ATTN_QUERY_TILE, tk=ATTN_KEY_BLOCK)
        xp = _merge_ffn(xp, oa.reshape(bsz * seq, v_w), ob.reshape(bsz * seq, sb_w), ga, gb,
                        *merge, tm=FFN_TOKEN_TILE)
        new_kv_p.append((
            jnp.transpose(ka_t.reshape(bsz, DA_HEADS, 2, DA_HEAD_DIM, seq), (0, 4, 1, 2, 3)),
            va.reshape(bsz, seq, DA_HEADS, DA_V_DIM),
            jnp.transpose(kb_t.reshape(bsz, SB_HEADS, SB_HEAD_DIM, seq), (0, 3, 1, 2)),
            jnp.transpose(vb_t.reshape(bsz, SB_HEADS, SB_HEAD_DIM, seq), (0, 3, 1, 2))))

        xs = _ffn_half(xs, *ffn1, tm=n_s)
        (qa, qb, ka_t, va, kb_t, vb_t, ga, gb, ka, kb, vb) = _proj(
            xs, row(norm2), w_in_l, w_t, *tabs_s, tm=n_s, sample=True)
        shs = lambda a: a.reshape(n_seq, n_tok, a.shape[-1])
        rows = 2 * DA_HEADS * n_tok
        ck_a = jnp.transpose(cache_diff_k[l], (0, 2, 3, 4, 1)).reshape(pool, qk_w, page)
        cv_a = cache_diff_v[l].reshape(pool, page * DA_HEADS, DA_V_DIM)
        ck_b = jnp.transpose(cache_sb_k[l], (0, 2, 3, 1)).reshape(pool, sb_w, page)
        cv_b = jnp.transpose(cache_sb_v[l], (0, 2, 3, 1)).reshape(pool, sb_w, page)
        oa = _sample_attention(
            functools.partial(_diff_sample_kernel, lam_init=lam_init),
            shs(qa), ka_t[0], va, (*lams, g_sub), ck_a, cv_a, page_table,
            reverse=False, out_block=(DA_HEADS * n_tok, DA_V_DIM),
            scratch=[pltpu.VMEM((rows, qk_w), BF16), pltpu.VMEM((rows, 1), F32),
                     pltpu.VMEM((rows, 1), F32), pltpu.VMEM((rows, DA_V_DIM), F32)],
            name="diff_attn_sample")
        oa = jnp.transpose(oa.reshape(n_seq, DA_HEADS, n_tok, DA_V_DIM), (0, 2, 1, 3))
        rows = SB_HEADS * n_tok
        ob = _sample_attention(
            _sb_sample_kernel, shs(qb), kb_t[0], vb_t[0], (), ck_b, cv_b, page_table,
            reverse=True, out_block=(n_tok, sb_w),
            scratch=[pltpu.VMEM((rows, sb_w), BF16), pltpu.VMEM((rows, 1), F32),
                     pltpu.VMEM((rows, sb_w), F32)],
            name="sb_attn_sample")
        xs = _merge_ffn(xs, oa.reshape(n_s, v_w), ob.reshape(n_s, sb_w), ga, gb, *merge, tm=n_s)
        new_kv_s.append((ka.reshape(n_seq, n_tok, DA_HEADS, 2, DA_HEAD_DIM),
                         va.reshape(n_seq, n_tok, DA_HEADS, DA_V_DIM),
                         kb.reshape(n_seq, n_tok, SB_HEADS, SB_HEAD_DIM),
                         vb.reshape(n_seq, n_tok, SB_HEADS, SB_HEAD_DIM)))

    stack = lambda items, i: jnp.stack([it[i] for it in items])
    return (xp.reshape(bsz, seq, d), xs.reshape(n_seq, n_tok, d),
            *(stack(new_kv_p, i) for i in range(4)),
            *(stack(new_kv_s, i) for i in range(4)))
```

```python
import functools
import math

import jax
import jax.numpy as jnp
from jax import lax
from jax.experimental import pallas as pl
from jax.experimental.pallas import tpu as pltpu

F32 = jnp.float32
BF16 = jnp.bfloat16

DA_HEADS = 4
DA_HEAD_DIM = 64
DA_V_DIM = 2 * DA_HEAD_DIM
SB_HEADS = 8
SB_HEAD_DIM = 64
ROPE_THETA = 500000.0
ROPE_DIM = DA_HEAD_DIM // 4
FFN_RES = 0.5
RMS_EPS = 1e-6
NEG_INF = -1e30

LANES = 128
SUBLANES = 8
VMEM_LIMIT_BYTES = 56 * 1024 * 1024

FFN_TOKEN_TILE = 512
FFN_CHUNK = 1408
PROJ_TOKEN_TILE = 256
ATTN_QUERY_TILE = 512
ATTN_KEY_BLOCK = 256
PAGES_PER_STEP = 16


def _const_spec(shape):
    zeros = (0,) * len(shape)
    return pl.BlockSpec(shape, lambda *_: zeros, pipeline_mode=pl.Buffered(1))


def _log2(n):
    assert n & (n - 1) == 0
    return n.bit_length() - 1


def _rmsnorm(x, g):
    ms = jnp.mean(x * x, axis=-1, keepdims=True)
    return x * lax.rsqrt(ms + RMS_EPS) * g


def _dot(a, b):
    return jnp.dot(a, b, preferred_element_type=F32)


def _dot_nt(a, b):
    return lax.dot_general(a, b, (((1,), (1,)), ((), ())), preferred_element_type=F32)


def _swiglu(xn, wg_ref, wu_ref, wd_ref):
    xb = xn.astype(BF16)
    d_ff = wg_ref.shape[1]
    chunk = FFN_CHUNK if d_ff % FFN_CHUNK == 0 else d_ff
    acc = None
    for c in range(d_ff // chunk):
        lo, hi = c * chunk, (c + 1) * chunk
        hg = _dot(xb, wg_ref[:, lo:hi])
        hu = _dot(xb, wu_ref[:, lo:hi])
        h = (hg * jax.nn.sigmoid(hg) * hu).astype(BF16)
        part = _dot(h, wd_ref[lo:hi, :])
        acc = part if acc is None else acc + part
    return acc


def _lam_from_refs(lq1_ref, lk1_ref, lq2_ref, lk2_ref, lam_init):
    a1 = jnp.sum(lq1_ref[...] * lk1_ref[...], axis=-1, keepdims=True)
    a2 = jnp.sum(lq2_ref[...] * lk2_ref[...], axis=-1, keepdims=True)
    return jnp.exp(a1) - jnp.exp(a2) + lam_init


LOG2_E = 1.4426950408889634


def _log2_keep(z):
    z2 = z * LOG2_E
    nz2 = z * (-LOG2_E)
    return jnp.minimum(nz2, 0.0) - jnp.log2(1.0 + jnp.exp2(jnp.minimum(z2, nz2))), z2


def _suffix_sum_matrix(n):
    j = lax.broadcasted_iota(jnp.int32, (2 * n, n), 0) & (n - 1)
    k = lax.broadcasted_iota(jnp.int32, (2 * n, n), 1)
    return jnp.where(j >= k, 1.0, 0.0).astype(BF16)


def _inclusive_suffix_sum(x, u2):
    hi = x.astype(BF16)
    lo = (x - hi.astype(F32)).astype(BF16)
    return _dot(jnp.concatenate([hi, lo], axis=1), u2)


def _ffn_half_kernel(x_ref, g_ref, wg_ref, wu_ref, wd_ref, o_ref):
    x = x_ref[...]
    o_ref[...] = x + FFN_RES * _swiglu(_rmsnorm(x, g_ref[...]), wg_ref, wu_ref, wd_ref)


def _ffn_half(x, g, wg, wu, wd, *, tm):
    n, d = x.shape
    d_ff = wg.shape[1]
    row = pl.BlockSpec((tm, d), lambda i: (i, 0))
    return pl.pallas_call(
        _ffn_half_kernel,
        grid=(n // tm,),
        in_specs=[row, _const_spec((1, d)), _const_spec((d, d_ff)), _const_spec((d, d_ff)),
                  _const_spec((d_ff, d))],
        out_specs=row,
        out_shape=jax.ShapeDtypeStruct((n, d), F32),
        compiler_params=pltpu.CompilerParams(
            dimension_semantics=("parallel",), vmem_limit_bytes=VMEM_LIMIT_BYTES),
        name="ffn_half",
    )(x, g, wg, wu, wd)


def _rope_rows(t, cos, sin_next, sin_prev):
    half = ROPE_DIM // 2
    groups = []
    for c in range(0, t.shape[1], LANES):
        x = t[:, c:c + LANES]
        nxt = pltpu.roll(x, LANES - half, axis=1)
        prv = pltpu.roll(x, half, axis=1)
        groups.append(x * cos + nxt * sin_next + prv * sin_prev)
    return jnp.concatenate(groups, axis=1)


def _rope_cols(t, cos_t, sin_t):
    half = ROPE_DIM // 2
    assert half == SUBLANES
    pieces = []
    for r in range(0, t.shape[0], DA_HEAD_DIM):
        x1 = t[r:r + half]
        x2 = t[r + half:r + 2 * half]
        pieces += [x1 * cos_t - x2 * sin_t, x2 * cos_t + x1 * sin_t,
                   t[r + 2 * half:r + DA_HEAD_DIM]]
    return jnp.concatenate(pieces, axis=0)


_QA, _KA, _VA, _QB, _KB, _VB, _GA, _GB = range(8)


def _proj_kernel(x_ref, g_ref, w_ref, wt_ref, cos_ref, sn_ref, sp_ref, cos_t_ref, sin_t_ref,
                 *outs, offs, da_scale, sb_scale, sample):
    xb = _rmsnorm(x_ref[...], g_ref[...]).astype(BF16)

    def seg(i):
        return _dot(xb, w_ref[:, offs[i]:offs[i + 1]])

    def seg_t(i):
        w = offs[_KA + 1] - offs[_KA]
        return _dot_nt(wt_ref[i * w:(i + 1) * w, :], xb)

    qa_o, qb_o, ka_t, va_o, kb_t, vb_t, ga_o, gb_o = outs[:8]
    rope_tabs = (cos_ref[...], sn_ref[...], sp_ref[...])
    qa_o[...] = (_rope_rows(seg(_QA), *rope_tabs) * da_scale).astype(qa_o.dtype)
    qb_o[...] = (seg(_QB) * sb_scale).astype(qb_o.dtype)
    ka = _rope_cols(seg_t(0), cos_t_ref[...], sin_t_ref[...])
    kb = seg_t(1)
    vb = seg_t(2)
    va = seg(_VA)
    ka_t[...] = ka
    va_o[...] = va
    kb_t[...] = kb
    vb_t[...] = vb
    ga_o[...] = seg(_GA)
    gb_o[...] = seg(_GB)
    if sample:
        ka_o, kb_o, vb_o = outs[8:]
        ka_o[...] = _rope_rows(seg(_KA), *rope_tabs)
        kb_o[...] = seg(_KB)
        vb_o[...] = seg(_VB)
    else:
        ka_h, va_h, kb_h, vb_h = outs[8:]
        ka_h[...] = ka.astype(BF16)
        va_h[...] = va.astype(BF16)
        kb_h[...] = kb.astype(BF16)
        vb_h[...] = vb.astype(BF16)


def _proj(x, g, w_in, w_t, tabs_rows, tabs_cols, *, tm, sample):
    n, d = x.shape
    qk_w = DA_HEADS * 2 * DA_HEAD_DIM
    v_w = DA_HEADS * DA_V_DIM
    sb_w = SB_HEADS * SB_HEAD_DIM
    assert qk_w == sb_w and w_t.shape[0] == 3 * qk_w
    widths = (qk_w, qk_w, v_w, sb_w, sb_w, sb_w, d, d)
    assert sum(widths) == w_in.shape[1]
    offs = tuple(sum(widths[:i]) for i in range(len(widths) + 1))
    period = tabs_rows[0].shape[0]
    pos_blocks = period // tm
    n_batch = n // period
    q_dtype = F32 if sample else BF16

    def row(w):
        return pl.BlockSpec((tm, w), lambda i: (i, 0))

    col = pl.BlockSpec((None, qk_w, tm), lambda i: (i // pos_blocks, 0, i % pos_blocks))
    tab_r = pl.BlockSpec((tm, LANES), lambda i: (i % pos_blocks, 0))
    tab_c = pl.BlockSpec((ROPE_DIM // 2, tm), lambda i: (0, i % pos_blocks))

    def rows(w, dt):
        return jax.ShapeDtypeStruct((n, w), dt), row(w)

    def cols(dt):
        return jax.ShapeDtypeStruct((n_batch, qk_w, period), dt), col

    outs = [rows(qk_w, q_dtype), rows(sb_w, q_dtype), cols(F32), rows(v_w, F32), cols(F32),
            cols(F32), rows(d, F32), rows(d, F32)]
    if sample:
        outs += [rows(qk_w, F32), rows(sb_w, F32), rows(sb_w, F32)]
    else:
        outs += [cols(BF16), rows(v_w, BF16), cols(BF16), cols(BF16)]
    kern = functools.partial(_proj_kernel, offs=offs, da_scale=DA_HEAD_DIM ** -0.5,
                             sb_scale=SB_HEAD_DIM ** -0.5, sample=sample)
    return pl.pallas_call(
        kern,
        grid=(n // tm,),
        in_specs=[row(d), _const_spec((1, d)), _const_spec(w_in.shape), _const_spec(w_t.shape),
                  tab_r, tab_r, tab_r, tab_c, tab_c],
        out_specs=[o[1] for o in outs],
        out_shape=[o[0] for o in outs],
        compiler_params=pltpu.CompilerParams(
            dimension_semantics=("parallel",), vmem_limit_bytes=VMEM_LIMIT_BYTES),
        name="in_proj",
    )(x, g, w_in, w_t, *tabs_rows, *tabs_cols)


def _merge_ffn_kernel(x_ref, oa_ref, ob_ref, ga_ref, gb_ref, wa_ref, wb_ref, wo_ref,
                      g3_ref, wg_ref, wu_ref, wd_ref, gf_ref, o_ref):
    ya = _dot(oa_ref[...].astype(BF16), wa_ref[...])
    yb = _dot(ob_ref[...].astype(BF16), wb_ref[...])
    m = jax.nn.sigmoid(ga_ref[...]) * ya + jax.nn.sigmoid(gb_ref[...]) * yb
    x = x_ref[...] + _dot(m.astype(BF16), wo_ref[...])
    y = x + FFN_RES * _swiglu(_rmsnorm(x, g3_ref[...]), wg_ref, wu_ref, wd_ref)
    o_ref[...] = _rmsnorm(y, gf_ref[...])


def _merge_ffn(x, oa, ob, ga, gb, wa, wb, wo, g3, wg, wu, wd, gf, *, tm):
    n, d = x.shape

    def row(w):
        return pl.BlockSpec((tm, w), lambda i: (i, 0))

    consts = [wa, wb, wo, g3, wg, wu, wd, gf]
    return pl.pallas_call(
        _merge_ffn_kernel,
        grid=(n // tm,),
        in_specs=[row(d), row(oa.shape[1]), row(ob.shape[1]), row(d), row(d)]
        + [_const_spec(c.shape) for c in consts],
        out_specs=row(d),
        out_shape=jax.ShapeDtypeStruct((n, d), F32),
        compiler_params=pltpu.CompilerParams(
            dimension_semantics=("parallel",), vmem_limit_bytes=VMEM_LIMIT_BYTES),
        name="merge_ffn",
    )(x, oa, ob, ga, gb, *consts)


def _stack_halves(q):
    lane = lax.broadcasted_iota(jnp.int32, q.shape, 1)
    zero = jnp.zeros_like(q)
    return jnp.concatenate([jnp.where(lane < LANES // 2, q, zero),
                            jnp.where(lane >= LANES // 2, q, zero)], axis=0)


def _causal_keep(shape, tq, key_offset, strict):
    row = lax.broadcasted_iota(jnp.int32, shape, 0) & (tq - 1)
    key = lax.broadcasted_iota(jnp.int32, shape, 1) + key_offset
    return key < row if strict else key <= row


def _lane_chunks(x):
    return [x[:, c:c + LANES] for c in range(0, x.shape[1], LANES)]


def _sweep_key_blocks(qi, produce, consume, bufs_a, bufs_b):
    def put(bufs, vals):
        for buf, val in zip(bufs, vals):
            buf[...] = val

    def get(bufs):
        return [buf[...] for buf in bufs]

    n_full = 2 * qi
    put(bufs_b, produce(n_full + 1, 1))
    put(bufs_a, produce(n_full, 0))
    consume(n_full + 1, *get(bufs_b))

    def body(i, carry):
        kb = n_full - 1 - 2 * i
        put(bufs_b, produce(kb, None))
        consume(kb + 1, *get(bufs_a))
        put(bufs_a, produce(kb - 1, None))
        consume(kb, *get(bufs_b))
        return carry

    lax.fori_loop(0, qi, body, 0)
    consume(0, *get(bufs_a))


def _diff_prompt_kernel(q_ref, kt_ref, v_ref, lq1_ref, lk1_ref, lq2_ref, lk2_ref, g_ref,
                        o_ref, mx_sc, l_sc, acc_sc, sa_sc, sb_sc, *, lam_init, tk):
    qi = pl.program_id(2)
    tq = q_ref.shape[0]
    qq = _stack_halves(q_ref[...])

    def scores(kb, causal_block):
        start = pl.multiple_of(kb * tk, tk)
        s = _dot(qq, kt_ref[:, pl.ds(start, tk)])
        if causal_block is not None:
            keep = _causal_keep(s.shape, tq, causal_block * tk, False)
            s = jnp.where(keep, s, NEG_INF)
        return (s,)

    mx_sc[...] = jnp.full(mx_sc.shape, NEG_INF, F32)

    def fold_max(kb, s):
        mx = mx_sc[...]
        for chunk in _lane_chunks(s):
            mx = jnp.maximum(mx, chunk)
        mx_sc[...] = mx

    _sweep_key_blocks(qi, scores, fold_max, (sa_sc,), (sb_sc,))
    m = jnp.max(mx_sc[...], axis=-1, keepdims=True)
    mx_sc[...] = jnp.broadcast_to(m, mx_sc.shape)

    l_sc[...] = jnp.zeros(l_sc.shape, F32)
    acc_sc[...] = jnp.zeros(acc_sc.shape, F32)

    def accumulate(kb, s):
        start = pl.multiple_of(kb * tk, tk)
        m_rep = mx_sc[...]
        ps = [jnp.exp(chunk - m_rep) for chunk in _lane_chunks(s)]
        l_new = l_sc[...]
        for p in ps:
            l_new = l_new + p
        l_sc[...] = l_new
        p = jnp.concatenate(ps, axis=1).astype(BF16)
        acc_sc[...] += _dot(p, v_ref[pl.ds(start, tk), :])

    _sweep_key_blocks(qi, scores, accumulate, (sa_sc,), (sb_sc,))

    lam = _lam_from_refs(lq1_ref, lk1_ref, lq2_ref, lk2_ref, lam_init)
    o = acc_sc[...] / jnp.sum(l_sc[...], axis=-1, keepdims=True)
    o = o[:tq] - lam * o[tq:]
    o_ref[...] = (_rmsnorm(o, g_ref[...]) * (1.0 - lam_init)).astype(o_ref.dtype)


def _diff_prompt(q, kt, v, lams, g, *, lam_init, tq, tk):
    b, s, w = q.shape
    heads = w // LANES
    assert tq & (tq - 1) == 0 and s % tq == 0 and tq == 2 * tk and tk % LANES == 0
    q_spec = pl.BlockSpec((None, tq, LANES), lambda bi, h, qi: (bi, qi, h))
    kt_spec = pl.BlockSpec((None, LANES, s), lambda bi, h, qi: (bi, h, 0))
    v_spec = pl.BlockSpec((None, s, LANES), lambda bi, h, qi: (bi, 0, h))
    small = [_const_spec(a.shape) for a in (*lams, g)]
    return pl.pallas_call(
        functools.partial(_diff_prompt_kernel, lam_init=lam_init, tk=tk),
        grid=(b, heads, s // tq),
        in_specs=[q_spec, kt_spec, v_spec] + small,
        out_specs=q_spec,
        out_shape=jax.ShapeDtypeStruct((b, s, w), BF16),
        scratch_shapes=[pltpu.VMEM((2 * tq, LANES), F32)] * 3
        + [pltpu.VMEM((2 * tq, tk), F32)] * 2,
        compiler_params=pltpu.CompilerParams(
            dimension_semantics=("parallel", "parallel", "arbitrary"),
            vmem_limit_bytes=VMEM_LIMIT_BYTES),
        name="diff_attn_prompt",
    )(q, kt, v, *lams, g)


def _sb_prompt_kernel(q_ref, kt_ref, vt_ref, o_ref, c_sc, acc_sc, ea_sc, ta_sc, eb_sc, tb_sc,
                      *, tk):
    qi = pl.program_id(2)
    tq = q_ref.shape[0]
    qq = _stack_halves(q_ref[...])
    u = _suffix_sum_matrix(tk)
    c_sc[...] = jnp.zeros(c_sc.shape, F32)
    acc_sc[...] = jnp.zeros(acc_sc.shape, F32)

    def log_weights(kb, causal_block):
        start = pl.multiple_of(kb * tk, tk)
        z = _dot(qq, kt_ref[:, pl.ds(start, tk)])
        lk, z2 = _log2_keep(z)
        if causal_block is not None:
            keep = _causal_keep(z.shape, tq, causal_block * tk, True)
            lk = jnp.where(keep, lk, 0.0)
            z2 = jnp.where(keep, z2, NEG_INF)
        cs = _inclusive_suffix_sum(lk, u)
        return z2 + cs, cs[:, :1]

    def accumulate(kb, e, total):
        start = pl.multiple_of(kb * tk, tk)
        a = jnp.exp2(e + c_sc[...])
        acc_sc[...] += _dot_nt(a.astype(BF16), vt_ref[:, pl.ds(start, tk)])
        c_sc[...] += total

    _sweep_key_blocks(qi, log_weights, accumulate, (ea_sc, ta_sc), (eb_sc, tb_sc))
    acc = acc_sc[...]
    lane = lax.broadcasted_iota(jnp.int32, (tq, LANES), 1)
    o_ref[...] = jnp.where(lane < LANES // 2, acc[:tq], acc[tq:]).astype(o_ref.dtype)


def _sb_prompt(q, kt, vt, *, tq, tk):
    b, s, w = q.shape
    groups = w // LANES
    assert tq & (tq - 1) == 0 and s % tq == 0 and tq == 2 * tk and tk % LANES == 0
    q_spec = pl.BlockSpec((None, tq, LANES), lambda bi, h, qi: (bi, qi, h))
    t_spec = pl.BlockSpec((None, LANES, s), lambda bi, h, qi: (bi, h, 0))
    block_bufs = [pltpu.VMEM((2 * tq, tk), F32), pltpu.VMEM((2 * tq, 1), F32)]
    return pl.pallas_call(
        functools.partial(_sb_prompt_kernel, tk=tk),
        grid=(b, groups, s // tq),
        in_specs=[q_spec, t_spec, t_spec],
        out_specs=q_spec,
        out_shape=jax.ShapeDtypeStruct((b, s, w), BF16),
        scratch_shapes=[pltpu.VMEM((2 * tq, 1), F32), pltpu.VMEM((2 * tq, LANES), F32)]
        + block_bufs + block_bufs,
        compiler_params=pltpu.CompilerParams(
            dimension_semantics=("parallel", "parallel", "arbitrary"),
            vmem_limit_bytes=VMEM_LIMIT_BYTES),
        name="sb_attn_prompt",
    )(q, kt, vt)


def _row_select(q, tok_of_row):
    out = jnp.zeros(tok_of_row.shape, q.dtype)
    for t in range(q.shape[0]):
        out = jnp.where(tok_of_row == t, jnp.broadcast_to(q[t:t + 1, :], out.shape), out)
    return out


def _new_token_mask(tok_of_row, n_tok, seq, strict):
    key = lax.broadcasted_iota(jnp.int32, tok_of_row.shape, 1)
    key_tok = key & (n_tok - 1)
    causal = key_tok < tok_of_row if strict else key_tok <= tok_of_row
    return ((key >> _log2(n_tok)) == seq) & causal


def _diff_sample_kernel(pt_ref, q_ref, knt_ref, vn_ref, lq1_ref, lk1_ref, lq2_ref, lk2_ref,
                        g_ref, *rest, n_pages_step, n_tok, lam_init):
    kt_refs = rest[:n_pages_step]
    v_refs = rest[n_pages_step:2 * n_pages_step]
    o_ref = rest[2 * n_pages_step]
    qbd_sc, m_sc, l_sc, acc_sc, s_sc = rest[2 * n_pages_step + 1:]
    b = pl.program_id(0)
    j = pl.program_id(1)
    rows, w = qbd_sc.shape
    page = kt_refs[0].shape[1]
    step_rows = n_pages_step * rows
    row_head = (lax.broadcasted_iota(jnp.int32, (rows, DA_V_DIM), 0) >> _log2(n_tok)) \
        & (DA_HEADS - 1)

    def scores(kts):
        qbd = qbd_sc[...]
        return [_dot(qbd, kt.astype(BF16)) for kt in kts]

    def attend(ss, v_heads):
        m_prev = m_sc[...]
        m_new = m_prev
        for s in ss:
            m_new = jnp.maximum(m_new, jnp.max(s, axis=-1, keepdims=True))
        alpha = jnp.exp(m_prev - m_new)
        l_new = alpha * l_sc[...]
        acc = alpha * acc_sc[...]
        for s, v_head in zip(ss, v_heads):
            p = jnp.exp(s - m_new)
            l_new = l_new + jnp.sum(p, axis=-1, keepdims=True)
            pb = p.astype(BF16)
            for h in range(DA_HEADS):
                pv = _dot(pb, v_head(h).astype(BF16))
                acc = acc + jnp.where(row_head == h, pv, 0.0)
        m_sc[...] = m_new
        l_sc[...] = l_new
        acc_sc[...] = acc

    @pl.when(j == 0)
    def _():
        row = lax.broadcasted_iota(jnp.int32, (rows, w), 0)
        lane = lax.broadcasted_iota(jnp.int32, (rows, w), 1)
        tok = row & (n_tok - 1)
        head_comp = ((row >> _log2(n_tok)) & (DA_HEADS - 1)) * 2 + (row >> _log2(n_tok * DA_HEADS))
        q_rows = _row_select(q_ref[...], tok)
        qbd = jnp.where((lane >> _log2(DA_HEAD_DIM)) == head_comp, q_rows, 0.0)
        qbd_sc[...] = qbd.astype(BF16)
        s_sc[pl.ds(step_rows, step_rows), :] = jnp.zeros((step_rows, page), F32)
        m_sc[...] = jnp.full(m_sc.shape, NEG_INF, F32)
        l_sc[...] = jnp.zeros(l_sc.shape, F32)
        acc_sc[...] = jnp.zeros(acc_sc.shape, F32)

    write_slot = pl.multiple_of((j & 1) * step_rows, step_rows)
    read_slot = pl.multiple_of(((j + 1) & 1) * step_rows, step_rows)
    prev = s_sc[pl.ds(read_slot, step_rows), :]
    s_sc[pl.ds(write_slot, step_rows), :] = jnp.concatenate(
        scores([r[...] for r in kt_refs]), axis=0)
    attend([prev[i * rows:(i + 1) * rows] for i in range(n_pages_step)],
           [lambda h, r=r: r[pl.ds(h, page, stride=DA_HEADS), :] for r in v_refs])

    @pl.when(j == 0)
    def _():
        m_sc[...] = jnp.full(m_sc.shape, NEG_INF, F32)
        l_sc[...] = jnp.zeros(l_sc.shape, F32)
        acc_sc[...] = jnp.zeros(acc_sc.shape, F32)
        n_new = knt_ref.shape[1]
        tok_n = lax.broadcasted_iota(jnp.int32, (rows, n_new), 0) & (n_tok - 1)
        mask = _new_token_mask(tok_n, n_tok, b, False)
        attend([jnp.where(mask, s, NEG_INF) for s in scores([knt_ref[...]])],
               [lambda h: vn_ref[:, h * DA_V_DIM:(h + 1) * DA_V_DIM]])

    @pl.when(j == pl.num_programs(1) - 1)
    def _():
        lam = _lam_from_refs(lq1_ref, lk1_ref, lq2_ref, lk2_ref, lam_init)
        o = acc_sc[...] / l_sc[...]
        o = o[:rows // 2] - lam * o[rows // 2:]
        o_ref[...] = _rmsnorm(o, g_ref[...]) * (1.0 - lam_init)


def _sb_sample_kernel(pt_ref, q_ref, knt_ref, vnt_ref, *rest, n_pages_step, n_tok):
    kt_refs = rest[:n_pages_step]
    vt_refs = rest[n_pages_step:2 * n_pages_step]
    o_ref = rest[2 * n_pages_step]
    qbd_sc, c_sc, acc_sc, e_sc, t_sc = rest[2 * n_pages_step + 1:]
    b = pl.program_id(0)
    j = pl.program_id(1)
    rows, w = qbd_sc.shape
    page = kt_refs[0].shape[1]
    step_rows = n_pages_step * rows
    u = _suffix_sum_matrix(page)

    def log_weights(kts, mask):
        qbd = qbd_sc[...]
        pairs = [_log2_keep(_dot(qbd, kt.astype(BF16))) for kt in kts]
        lk = jnp.concatenate([lk for lk, _ in pairs], axis=0)
        z2 = jnp.concatenate([z2 for _, z2 in pairs], axis=0)
        if mask is not None:
            lk = jnp.where(mask, lk, 0.0)
            z2 = jnp.where(mask, z2, NEG_INF)
        cs = _inclusive_suffix_sum(lk, u)
        return z2 + cs, cs[:, :1]

    def attend(e, totals, vts):
        c = c_sc[...]
        acc = acc_sc[...]
        for i, vt in enumerate(vts):
            a = jnp.exp2(e[i * rows:(i + 1) * rows] + c)
            acc = acc + _dot_nt(a.astype(BF16), vt.astype(BF16))
            c = c + totals[i * rows:(i + 1) * rows]
        c_sc[...] = c
        acc_sc[...] = acc

    @pl.when(j == 0)
    def _():
        row = lax.broadcasted_iota(jnp.int32, (rows, w), 0)
        lane = lax.broadcasted_iota(jnp.int32, (rows, w), 1)
        q_rows = _row_select(q_ref[...], row >> _log2(SB_HEADS))
        qbd = jnp.where((lane >> _log2(SB_HEAD_DIM)) == (row & (SB_HEADS - 1)), q_rows, 0.0)
        qbd_sc[...] = qbd.astype(BF16)
        e_sc[pl.ds(step_rows, step_rows), :] = jnp.zeros((step_rows, page), F32)
        t_sc[pl.ds(step_rows, step_rows), :] = jnp.zeros((step_rows, 1), F32)
        c_sc[...] = jnp.zeros(c_sc.shape, F32)
        acc_sc[...] = jnp.zeros(acc_sc.shape, F32)

    write_slot = pl.multiple_of((j & 1) * step_rows, step_rows)
    read_slot = pl.multiple_of(((j + 1) & 1) * step_rows, step_rows)
    e_prev = e_sc[pl.ds(read_slot, step_rows), :]
    t_prev = t_sc[pl.ds(read_slot, step_rows), :]
    e_new, t_new = log_weights([r[...] for r in kt_refs], None)
    e_sc[pl.ds(write_slot, step_rows), :] = e_new
    t_sc[pl.ds(write_slot, step_rows), :] = t_new
    attend(e_prev, t_prev, [r[...] for r in vt_refs])

    @pl.when(j == 0)
    def _():
        c_sc[...] = jnp.zeros(c_sc.shape, F32)
        acc_sc[...] = jnp.zeros(acc_sc.shape, F32)
        n_new = knt_ref.shape[1]
        assert n_new == page
        tok_n = lax.broadcasted_iota(jnp.int32, (rows, n_new), 0) >> _log2(SB_HEADS)
        mask = _new_token_mask(tok_n, n_tok, b, True)
        attend(*log_weights([knt_ref[...]], mask), [vnt_ref[...]])

    @pl.when(j == pl.num_programs(1) - 1)
    def _():
        acc = acc_sc[...]
        h = lax.broadcasted_iota(jnp.int32, (SB_HEADS, w), 0)
        lane = lax.broadcasted_iota(jnp.int32, (SB_HEADS, w), 1)
        keep = (lane >> _log2(SB_HEAD_DIM)) == h
        for t in range(n_tok):
            tile = acc[t * SB_HEADS:(t + 1) * SB_HEADS]
            o_ref[t:t + 1, :] = jnp.sum(jnp.where(keep, tile, 0.0), axis=0, keepdims=True)


def _diff_sample_scratch(n_tok, page, per_step):
    rows = 2 * DA_HEADS * n_tok
    return [pltpu.VMEM((rows, DA_HEADS * 2 * DA_HEAD_DIM), BF16),
            pltpu.VMEM((rows, 1), F32), pltpu.VMEM((rows, 1), F32),
            pltpu.VMEM((rows, DA_V_DIM), F32),
            pltpu.VMEM((2 * per_step * rows, page), F32)]


def _sb_sample_scratch(n_tok, page, per_step):
    rows = SB_HEADS * n_tok
    w = SB_HEADS * SB_HEAD_DIM
    return [pltpu.VMEM((rows, w), BF16),
            pltpu.VMEM((rows, 1), F32), pltpu.VMEM((rows, w), F32),
            pltpu.VMEM((2 * per_step * rows, page), F32),
            pltpu.VMEM((2 * per_step * rows, 1), F32)]


def _page_index(b, j, pt_ref, *, slot, per_step, n_pages, reverse, lag):
    step = jnp.clip(j - lag, 0, n_pages // per_step - 1)
    p = step * per_step + slot
    if reverse:
        p = n_pages - 1 - p
    return (pt_ref[b, p], 0, 0)


def _sample_attention(kern, q, new_k, new_v, extras, cache_k, cache_v, page_table, *,
                      reverse, out_block, scratch, name):
    n_seq, n_tok, w = q.shape
    n_pages = page_table.shape[1]
    per_step = PAGES_PER_STEP
    assert n_pages % per_step == 0
    seq_spec = pl.BlockSpec((None, n_tok, w), lambda b, j, pt: (b, 0, 0))
    out_spec = pl.BlockSpec((None, *out_block), lambda b, j, pt: (b, 0, 0))

    def page_specs(cache, lag):
        return [pl.BlockSpec((None, *cache.shape[1:]),
                             functools.partial(_page_index, slot=i, per_step=per_step,
                                               n_pages=n_pages, reverse=reverse, lag=lag))
                for i in range(per_step)]

    grid_spec = pltpu.PrefetchScalarGridSpec(
        num_scalar_prefetch=1,
        grid=(n_seq, n_pages // per_step + 1),
        in_specs=[seq_spec] + [_const_spec(e.shape) for e in (new_k, new_v, *extras)]
        + page_specs(cache_k, 0) + page_specs(cache_v, 1),
        out_specs=out_spec,
        scratch_shapes=scratch(per_step),
    )
    return pl.pallas_call(
        functools.partial(kern, n_pages_step=per_step, n_tok=n_tok),
        grid_spec=grid_spec,
        out_shape=jax.ShapeDtypeStruct((n_seq, *out_block), F32),
        compiler_params=pltpu.CompilerParams(
            dimension_semantics=("parallel", "arbitrary"),
            vmem_limit_bytes=VMEM_LIMIT_BYTES),
        name=name,
    )(page_table, q, new_k, new_v, *extras, *([cache_k] * per_step), *([cache_v] * per_step))


def _rope_tables(pos):
    half = ROPE_DIM // 2
    inv_freq = ROPE_THETA ** (-jnp.arange(0, ROPE_DIM, 2, dtype=F32) / ROPE_DIM)
    ang = pos.astype(F32)[:, None] * inv_freq[None, :]
    cos, sin = jnp.cos(ang), jnp.sin(ang)
    n = pos.shape[0]
    rest = DA_HEAD_DIM - ROPE_DIM
    zeros_h = jnp.zeros((n, half), F32)
    cos_r = jnp.concatenate([cos, cos, jnp.ones((n, rest), F32)], axis=1)
    sin_next = jnp.concatenate([-sin, zeros_h, jnp.zeros((n, rest), F32)], axis=1)
    sin_prev = jnp.concatenate([zeros_h, sin, jnp.zeros((n, rest), F32)], axis=1)
    reps = LANES // DA_HEAD_DIM
    return (tuple(jnp.tile(t, (1, reps)) for t in (cos_r, sin_next, sin_prev)),
            (cos.T, sin.T))


def kernel(x_prompt, x_sample, cache_diff_k, cache_diff_v, cache_sb_k, cache_sb_v, page_table,
           norm1, w_ffn1_gate, w_ffn1_up, w_ffn1_down, norm2, w_in,
           lambda_q1, lambda_k1, lambda_q2, lambda_k2, subln_g,
           w_branch_a, w_branch_b, w_out, norm3, w_ffn2_gate, w_ffn2_up, w_ffn2_down, norm_f):
    bsz, seq, d = x_prompt.shape
    n_seq, n_tok, _ = x_sample.shape
    depth = norm1.shape[0]
    assert depth == 1, "the final norm is fused into the last layer's second FFN"
    pool, page = cache_diff_k.shape[1:3]
    past_len = page_table.shape[1] * page
    qk_w = DA_HEADS * 2 * DA_HEAD_DIM
    v_w = DA_HEADS * DA_V_DIM
    sb_w = SB_HEADS * SB_HEAD_DIM
    n_s = n_seq * n_tok

    xp = x_prompt.reshape(bsz * seq, d)
    xs = x_sample.reshape(n_s, d)
    tabs_p = _rope_tables(jnp.arange(seq))
    tabs_s = _rope_tables(jnp.tile(past_len + jnp.arange(n_tok), n_seq))
    gf = norm_f.reshape(1, d)
    new_kv_p, new_kv_s = [], []

    for l in range(depth):
        lam_init = 0.8 - 0.6 * math.exp(-0.3 * l)
        row = lambda a: a[l][None]
        cast = lambda a: a[l].astype(BF16)
        lams = tuple(row(a) for a in (lambda_q1, lambda_k1, lambda_q2, lambda_k2))
        ffn1 = (row(norm1), cast(w_ffn1_gate), cast(w_ffn1_up), cast(w_ffn1_down))
        merge = (cast(w_branch_a), cast(w_branch_b), cast(w_out), row(norm3),
                 cast(w_ffn2_gate), cast(w_ffn2_up), cast(w_ffn2_down), gf)
        w_in_l = cast(w_in)
        w_t = jnp.concatenate(
            [w_in_l[:, o:o + qk_w].T for o in (qk_w, 2 * qk_w + v_w + sb_w,
                                               2 * qk_w + v_w + 2 * sb_w)], axis=0)
        g_sub = row(subln_g)

        xp = _ffn_half(xp, *ffn1, tm=FFN_TOKEN_TILE)
        (qa, qb, ka_t, va, kb_t, vb_t, ga, gb, ka_h, va_h, kb_h, vb_h) = _proj(
            xp, row(norm2), w_in_l, w_t, *tabs_p, tm=PROJ_TOKEN_TILE, sample=False)
        shp = lambda a: a.reshape(bsz, seq, a.shape[-1])
        oa = _diff_prompt(shp(qa), ka_h, shp(va_h), lams, g_sub, lam_init=lam_init,
                          tq=ATTN_QUERY_TILE, tk=ATTN_KEY_BLOCK)
        ob = _sb_prompt(shp(qb), kb_h, vb_h, tq=ATTN_QUERY_TILE, tk=ATTN_KEY_BLOCK)
        xp = _merge_ffn(xp, oa.reshape(bsz * seq, v_w), ob.reshape(bsz * seq, sb_w), ga, gb,
                        *merge, tm=FFN_TOKEN_TILE)
        new_kv_p.append((
            jnp.transpose(ka_t.reshape(bsz, DA_HEADS, 2, DA_HEAD_DIM, seq), (0, 4, 1, 2, 3)),
            va.reshape(bsz, seq, DA_HEADS, DA_V_DIM),
            jnp.transpose(kb_t.reshape(bsz, SB_HEADS, SB_HEAD_DIM, seq), (0, 3, 1, 2)),
            jnp.transpose(vb_t.reshape(bsz, SB_HEADS, SB_HEAD_DIM, seq), (0, 3, 1, 2))))

        xs = _ffn_half(xs, *ffn1, tm=n_s)
        (qa, qb, ka_t, va, kb_t, vb_t, ga, gb, ka, kb, vb) = _proj(
            xs, row(norm2), w_in_l, w_t, *tabs_s, tm=n_s, sample=True)
        shs = lambda a: a.reshape(n_seq, n_tok, a.shape[-1])
        ck_a = jnp.transpose(cache_diff_k[l], (0, 2, 3, 4, 1)).reshape(pool, qk_w, page)
        cv_a = cache_diff_v[l].reshape(pool, page * DA_HEADS, DA_V_DIM)
        ck_b = jnp.transpose(cache_sb_k[l], (0, 2, 3, 1)).reshape(pool, sb_w, page)
        cv_b = jnp.transpose(cache_sb_v[l], (0, 2, 3, 1)).reshape(pool, sb_w, page)
        oa = _sample_attention(
            functools.partial(_diff_sample_kernel, lam_init=lam_init),
            shs(qa), ka_t[0], va, (*lams, g_sub), ck_a, cv_a, page_table,
            reverse=False, out_block=(DA_HEADS * n_tok, DA_V_DIM),
            scratch=functools.partial(_diff_sample_scratch, n_tok, page),
            name="diff_attn_sample")
        oa = jnp.transpose(oa.reshape(n_seq, DA_HEADS, n_tok, DA_V_DIM), (0, 2, 1, 3))
        ob = _sample_attention(
            _sb_sample_kernel, shs(qb), kb_t[0], vb_t[0], (), ck_b, cv_b, page_table,
            reverse=True, out_block=(n_tok, sb_w),
            scratch=functools.partial(_sb_sample_scratch, n_tok, page),
            name="sb_attn_sample")
        xs = _merge_ffn(xs, oa.reshape(n_s, v_w), ob.reshape(n_s, sb_w), ga, gb, *merge, tm=n_s)
        new_kv_s.append((ka.reshape(n_seq, n_tok, DA_HEADS, 2, DA_HEAD_DIM),
                         va.reshape(n_seq, n_tok, DA_HEADS, DA_V_DIM),
                         kb.reshape(n_seq, n_tok, SB_HEADS, SB_HEAD_DIM),
                         vb.reshape(n_seq, n_tok, SB_HEADS, SB_HEAD_DIM)))

    stack = lambda items, i: jnp.stack([it[i] for it in items])
    return (xp.reshape(bsz, seq, d), xs.reshape(n_seq, n_tok, d),
            *(stack(new_kv_p, i) for i in range(4)),
            *(stack(new_kv_s, i) for i in range(4)))
```

```python
import functools
import math

import jax
import jax.numpy as jnp
from jax import lax
from jax.experimental import pallas as pl
from jax.experimental.pallas import tpu as pltpu

F32 = jnp.float32
BF16 = jnp.bfloat16

DA_HEADS = 4
DA_HEAD_DIM = 64
DA_V_DIM = 2 * DA_HEAD_DIM
SB_HEADS = 8
SB_HEAD_DIM = 64
ROPE_THETA = 500000.0
ROPE_DIM = DA_HEAD_DIM // 4
FFN_RES = 0.5
RMS_EPS = 1e-6
NEG_INF = -1e30

LANES = 128
SUBLANES = 8
VMEM_LIMIT_BYTES = 56 * 1024 * 1024

FFN_TOKEN_TILE = 512
FFN_CHUNK = 1408
PROJ_TOKEN_TILE = 256
ATTN_QUERY_TILE = 512
ATTN_KEY_BLOCK = 256
PAGES_PER_STEP = 32


def _const_spec(shape):
    zeros = (0,) * len(shape)
    return pl.BlockSpec(shape, lambda *_: zeros, pipeline_mode=pl.Buffered(1))


def _log2(n):
    assert n & (n - 1) == 0
    return n.bit_length() - 1


def _rmsnorm(x, g):
    ms = jnp.mean(x * x, axis=-1, keepdims=True)
    return x * lax.rsqrt(ms + RMS_EPS) * g


def _dot(a, b):
    return jnp.dot(a, b, preferred_element_type=F32)


def _dot_nt(a, b):
    return lax.dot_general(a, b, (((1,), (1,)), ((), ())), preferred_element_type=F32)


def _swiglu(xn, wg_ref, wu_ref, wd_ref):
    xb = xn.astype(BF16)
    d_ff = wg_ref.shape[1]
    chunk = FFN_CHUNK if d_ff % FFN_CHUNK == 0 else d_ff
    acc = None
    for c in range(d_ff // chunk):
        lo, hi = c * chunk, (c + 1) * chunk
        hg = _dot(xb, wg_ref[:, lo:hi])
        hu = _dot(xb, wu_ref[:, lo:hi])
        h = (hg * jax.nn.sigmoid(hg) * hu).astype(BF16)
        part = _dot(h, wd_ref[lo:hi, :])
        acc = part if acc is None else acc + part
    return acc


def _lam_from_refs(lq1_ref, lk1_ref, lq2_ref, lk2_ref, lam_init):
    a1 = jnp.sum(lq1_ref[...] * lk1_ref[...], axis=-1, keepdims=True)
    a2 = jnp.sum(lq2_ref[...] * lk2_ref[...], axis=-1, keepdims=True)
    return jnp.exp(a1) - jnp.exp(a2) + lam_init


LOG2_E = 1.4426950408889634
DEAD_LOG2_WEIGHT = -150.0


def _log2_keep(z):
    z2 = z * LOG2_E
    nz2 = z * (-LOG2_E)
    return jnp.minimum(nz2, 0.0) - jnp.log2(1.0 + jnp.exp2(jnp.minimum(z2, nz2))), z2


def _suffix_sum_matrix(n):
    j = lax.broadcasted_iota(jnp.int32, (2 * n, n), 0) & (n - 1)
    k = lax.broadcasted_iota(jnp.int32, (2 * n, n), 1)
    return jnp.where(j >= k, 1.0, 0.0).astype(BF16)


def _inclusive_suffix_sum(x, u2):
    hi = x.astype(BF16)
    lo = (x - hi.astype(F32)).astype(BF16)
    return _dot(jnp.concatenate([hi, lo], axis=1), u2)


def _ffn_half_kernel(x_ref, g_ref, wg_ref, wu_ref, wd_ref, o_ref):
    x = x_ref[...]
    o_ref[...] = x + FFN_RES * _swiglu(_rmsnorm(x, g_ref[...]), wg_ref, wu_ref, wd_ref)


def _ffn_half(x, g, wg, wu, wd, *, tm):
    n, d = x.shape
    d_ff = wg.shape[1]
    row = pl.BlockSpec((tm, d), lambda i: (i, 0))
    return pl.pallas_call(
        _ffn_half_kernel,
        grid=(n // tm,),
        in_specs=[row, _const_spec((1, d)), _const_spec((d, d_ff)), _const_spec((d, d_ff)),
                  _const_spec((d_ff, d))],
        out_specs=row,
        out_shape=jax.ShapeDtypeStruct((n, d), F32),
        compiler_params=pltpu.CompilerParams(
            dimension_semantics=("parallel",), vmem_limit_bytes=VMEM_LIMIT_BYTES),
        name="ffn_half",
    )(x, g, wg, wu, wd)


def _rope_rows(t, cos, sin_next, sin_prev):
    half = ROPE_DIM // 2
    groups = []
    for c in range(0, t.shape[1], LANES):
        x = t[:, c:c + LANES]
        nxt = pltpu.roll(x, LANES - half, axis=1)
        prv = pltpu.roll(x, half, axis=1)
        groups.append(x * cos + nxt * sin_next + prv * sin_prev)
    return jnp.concatenate(groups, axis=1)


def _rope_cols(t, cos_t, sin_t):
    half = ROPE_DIM // 2
    assert half == SUBLANES
    pieces = []
    for r in range(0, t.shape[0], DA_HEAD_DIM):
        x1 = t[r:r + half]
        x2 = t[r + half:r + 2 * half]
        pieces += [x1 * cos_t - x2 * sin_t, x2 * cos_t + x1 * sin_t,
                   t[r + 2 * half:r + DA_HEAD_DIM]]
    return jnp.concatenate(pieces, axis=0)


_QA, _KA, _VA, _QB, _KB, _VB, _GA, _GB = range(8)


def _proj_kernel(x_ref, g_ref, w_ref, wt_ref, cos_ref, sn_ref, sp_ref, cos_t_ref, sin_t_ref,
                 *outs, offs, da_scale, sb_scale, sample):
    xb = _rmsnorm(x_ref[...], g_ref[...]).astype(BF16)

    def seg(i):
        return _dot(xb, w_ref[:, offs[i]:offs[i + 1]])

    def seg_t(i):
        w = offs[_KA + 1] - offs[_KA]
        return _dot_nt(wt_ref[i * w:(i + 1) * w, :], xb)

    qa_o, qb_o, ka_t, va_o, kb_t, vb_t, ga_o, gb_o = outs[:8]
    rope_tabs = (cos_ref[...], sn_ref[...], sp_ref[...])
    qa_o[...] = (_rope_rows(seg(_QA), *rope_tabs) * da_scale).astype(qa_o.dtype)
    qb_o[...] = (seg(_QB) * sb_scale).astype(qb_o.dtype)
    ka = _rope_cols(seg_t(0), cos_t_ref[...], sin_t_ref[...])
    kb = seg_t(1)
    vb = seg_t(2)
    va = seg(_VA)
    ka_t[...] = ka
    va_o[...] = va
    kb_t[...] = kb
    vb_t[...] = vb
    ga_o[...] = seg(_GA)
    gb_o[...] = seg(_GB)
    if sample:
        ka_o, kb_o, vb_o = outs[8:]
        ka_o[...] = _rope_rows(seg(_KA), *rope_tabs)
        kb_o[...] = seg(_KB)
        vb_o[...] = seg(_VB)
    else:
        ka_h, va_h, kb_h, vb_h = outs[8:]
        ka_h[...] = ka.astype(BF16)
        va_h[...] = va.astype(BF16)
        kb_h[...] = kb.astype(BF16)
        vb_h[...] = vb.astype(BF16)


def _proj(x, g, w_in, w_t, tabs_rows, tabs_cols, *, tm, sample):
    n, d = x.shape
    qk_w = DA_HEADS * 2 * DA_HEAD_DIM
    v_w = DA_HEADS * DA_V_DIM
    sb_w = SB_HEADS * SB_HEAD_DIM
    assert qk_w == sb_w and w_t.shape[0] == 3 * qk_w
    widths = (qk_w, qk_w, v_w, sb_w, sb_w, sb_w, d, d)
    assert sum(widths) == w_in.shape[1]
    offs = tuple(sum(widths[:i]) for i in range(len(widths) + 1))
    period = tabs_rows[0].shape[0]
    pos_blocks = period // tm
    n_batch = n // period
    q_dtype = F32 if sample else BF16

    def row(w):
        return pl.BlockSpec((tm, w), lambda i: (i, 0))

    col = pl.BlockSpec((None, qk_w, tm), lambda i: (i // pos_blocks, 0, i % pos_blocks))
    tab_r = pl.BlockSpec((tm, LANES), lambda i: (i % pos_blocks, 0))
    tab_c = pl.BlockSpec((ROPE_DIM // 2, tm), lambda i: (0, i % pos_blocks))

    def rows(w, dt):
        return jax.ShapeDtypeStruct((n, w), dt), row(w)

    def cols(dt):
        return jax.ShapeDtypeStruct((n_batch, qk_w, period), dt), col

    outs = [rows(qk_w, q_dtype), rows(sb_w, q_dtype), cols(F32), rows(v_w, F32), cols(F32),
            cols(F32), rows(d, F32), rows(d, F32)]
    if sample:
        outs += [rows(qk_w, F32), rows(sb_w, F32), rows(sb_w, F32)]
    else:
        outs += [cols(BF16), rows(v_w, BF16), cols(BF16), cols(BF16)]
    kern = functools.partial(_proj_kernel, offs=offs, da_scale=DA_HEAD_DIM ** -0.5,
                             sb_scale=SB_HEAD_DIM ** -0.5, sample=sample)
    return pl.pallas_call(
        kern,
        grid=(n // tm,),
        in_specs=[row(d), _const_spec((1, d)), _const_spec(w_in.shape), _const_spec(w_t.shape),
                  tab_r, tab_r, tab_r, tab_c, tab_c],
        out_specs=[o[1] for o in outs],
        out_shape=[o[0] for o in outs],
        compiler_params=pltpu.CompilerParams(
            dimension_semantics=("parallel",), vmem_limit_bytes=VMEM_LIMIT_BYTES),
        name="in_proj",
    )(x, g, w_in, w_t, *tabs_rows, *tabs_cols)


def _merge_ffn_kernel(x_ref, oa_ref, ob_ref, ga_ref, gb_ref, wa_ref, wb_ref, wo_ref,
                      g3_ref, wg_ref, wu_ref, wd_ref, gf_ref, o_ref):
    ya = _dot(oa_ref[...].astype(BF16), wa_ref[...])
    yb = _dot(ob_ref[...].astype(BF16), wb_ref[...])
    m = jax.nn.sigmoid(ga_ref[...]) * ya + jax.nn.sigmoid(gb_ref[...]) * yb
    x = x_ref[...] + _dot(m.astype(BF16), wo_ref[...])
    y = x + FFN_RES * _swiglu(_rmsnorm(x, g3_ref[...]), wg_ref, wu_ref, wd_ref)
    o_ref[...] = _rmsnorm(y, gf_ref[...])


def _merge_ffn(x, oa, ob, ga, gb, wa, wb, wo, g3, wg, wu, wd, gf, *, tm):
    n, d = x.shape

    def row(w):
        return pl.BlockSpec((tm, w), lambda i: (i, 0))

    consts = [wa, wb, wo, g3, wg, wu, wd, gf]
    return pl.pallas_call(
        _merge_ffn_kernel,
        grid=(n // tm,),
        in_specs=[row(d), row(oa.shape[1]), row(ob.shape[1]), row(d), row(d)]
        + [_const_spec(c.shape) for c in consts],
        out_specs=row(d),
        out_shape=jax.ShapeDtypeStruct((n, d), F32),
        compiler_params=pltpu.CompilerParams(
            dimension_semantics=("parallel",), vmem_limit_bytes=VMEM_LIMIT_BYTES),
        name="merge_ffn",
    )(x, oa, ob, ga, gb, *consts)


def _stack_halves(q):
    lane = lax.broadcasted_iota(jnp.int32, q.shape, 1)
    zero = jnp.zeros_like(q)
    return jnp.concatenate([jnp.where(lane < LANES // 2, q, zero),
                            jnp.where(lane >= LANES // 2, q, zero)], axis=0)


def _causal_keep(shape, tq, key_offset, strict):
    row = lax.broadcasted_iota(jnp.int32, shape, 0) & (tq - 1)
    key = lax.broadcasted_iota(jnp.int32, shape, 1) + key_offset
    return key < row if strict else key <= row


def _lane_chunks(x):
    return [x[:, c:c + LANES] for c in range(0, x.shape[1], LANES)]


def _sweep_key_blocks(qi, produce, consume, bufs_a, bufs_b, more_to_come=None):
    def put(bufs, vals):
        for buf, val in zip(bufs, vals):
            buf[...] = val

    def get(bufs):
        return [buf[...] for buf in bufs]

    n_full = 2 * qi
    put(bufs_b, produce(n_full + 1, 1))
    put(bufs_a, produce(n_full, 0))
    consume(n_full + 1, *get(bufs_b))

    def pair(i):
        kb = n_full - 1 - 2 * i
        put(bufs_b, produce(kb, None))
        consume(kb + 1, *get(bufs_a))
        put(bufs_a, produce(kb - 1, None))
        consume(kb, *get(bufs_b))

    if more_to_come is None:
        def body(i, carry):
            pair(i)
            return carry

        pairs_done = lax.fori_loop(0, qi, body, qi)
    else:
        def cond(state):
            i, go_on = state
            return jnp.logical_and(i < qi, go_on > 0)

        def body(state):
            i, _ = state
            pair(i)
            return i + 1, more_to_come()

        pairs_done, _ = lax.while_loop(cond, body, (jnp.int32(0), more_to_come()))
    consume(n_full - 2 * pairs_done, *get(bufs_a))


def _diff_prompt_kernel(q_ref, kt_ref, v_ref, lq1_ref, lk1_ref, lq2_ref, lk2_ref, g_ref,
                        o_ref, mx_sc, l_sc, acc_sc, sa_sc, sb_sc, *, lam_init, tk):
    qi = pl.program_id(2)
    tq = q_ref.shape[0]
    qq = _stack_halves(q_ref[...])

    def scores(kb, causal_block):
        start = pl.multiple_of(kb * tk, tk)
        s = _dot(qq, kt_ref[:, pl.ds(start, tk)])
        if causal_block is not None:
            keep = _causal_keep(s.shape, tq, causal_block * tk, False)
            s = jnp.where(keep, s, NEG_INF)
        return (s,)

    mx_sc[...] = jnp.full(mx_sc.shape, NEG_INF, F32)

    def fold_max(kb, s):
        mx = mx_sc[...]
        for chunk in _lane_chunks(s):
            mx = jnp.maximum(mx, chunk)
        mx_sc[...] = mx

    _sweep_key_blocks(qi, scores, fold_max, (sa_sc,), (sb_sc,))
    m = jnp.max(mx_sc[...], axis=-1, keepdims=True)
    mx_sc[...] = jnp.broadcast_to(m, mx_sc.shape)

    l_sc[...] = jnp.zeros(l_sc.shape, F32)
    acc_sc[...] = jnp.zeros(acc_sc.shape, F32)

    def accumulate(kb, s):
        start = pl.multiple_of(kb * tk, tk)
        m_rep = mx_sc[...]
        ps = [jnp.exp(chunk - m_rep) for chunk in _lane_chunks(s)]
        l_new = l_sc[...]
        for p in ps:
            l_new = l_new + p
        l_sc[...] = l_new
        p = jnp.concatenate(ps, axis=1).astype(BF16)
        acc_sc[...] += _dot(p, v_ref[pl.ds(start, tk), :])

    _sweep_key_blocks(qi, scores, accumulate, (sa_sc,), (sb_sc,))

    lam = _lam_from_refs(lq1_ref, lk1_ref, lq2_ref, lk2_ref, lam_init)
    o = acc_sc[...] / jnp.sum(l_sc[...], axis=-1, keepdims=True)
    o = o[:tq] - lam * o[tq:]
    o_ref[...] = (_rmsnorm(o, g_ref[...]) * (1.0 - lam_init)).astype(o_ref.dtype)


def _diff_prompt(q, kt, v, lams, g, *, lam_init, tq, tk):
    b, s, w = q.shape
    heads = w // LANES
    assert tq & (tq - 1) == 0 and s % tq == 0 and tq == 2 * tk and tk % LANES == 0
    q_spec = pl.BlockSpec((None, tq, LANES), lambda bi, h, qi: (bi, qi, h))
    kt_spec = pl.BlockSpec((None, LANES, s), lambda bi, h, qi: (bi, h, 0))
    v_spec = pl.BlockSpec((None, s, LANES), lambda bi, h, qi: (bi, 0, h))
    small = [_const_spec(a.shape) for a in (*lams, g)]
    return pl.pallas_call(
        functools.partial(_diff_prompt_kernel, lam_init=lam_init, tk=tk),
        grid=(b, heads, s // tq),
        in_specs=[q_spec, kt_spec, v_spec] + small,
        out_specs=q_spec,
        out_shape=jax.ShapeDtypeStruct((b, s, w), BF16),
        scratch_shapes=[pltpu.VMEM((2 * tq, LANES), F32)] * 3
        + [pltpu.VMEM((2 * tq, tk), F32)] * 2,
        compiler_params=pltpu.CompilerParams(
            dimension_semantics=("parallel", "parallel", "arbitrary"),
            vmem_limit_bytes=VMEM_LIMIT_BYTES),
        name="diff_attn_prompt",
    )(q, kt, v, *lams, g)


def _sb_prompt_kernel(q_ref, kt_ref, vt_ref, o_ref, c_sc, acc_sc, ea_sc, ta_sc, eb_sc, tb_sc,
                      *, tk):
    qi = pl.program_id(2)
    tq = q_ref.shape[0]
    qq = _stack_halves(q_ref[...])
    u = _suffix_sum_matrix(tk)
    c_sc[...] = jnp.zeros(c_sc.shape, F32)
    acc_sc[...] = jnp.zeros(acc_sc.shape, F32)

    def log_weights(kb, causal_block):
        start = pl.multiple_of(kb * tk, tk)
        z = _dot(qq, kt_ref[:, pl.ds(start, tk)])
        lk, z2 = _log2_keep(z)
        if causal_block is not None:
            keep = _causal_keep(z.shape, tq, causal_block * tk, True)
            lk = jnp.where(keep, lk, 0.0)
            z2 = jnp.where(keep, z2, NEG_INF)
        cs = _inclusive_suffix_sum(lk, u)
        return z2 + cs, cs[:, :1]

    def accumulate(kb, e, total):
        start = pl.multiple_of(kb * tk, tk)
        a = jnp.exp2(e + c_sc[...])
        acc_sc[...] += _dot_nt(a.astype(BF16), vt_ref[:, pl.ds(start, tk)])
        c_sc[...] += total

    def any_weight_left():
        return (jnp.max(c_sc[...]) > DEAD_LOG2_WEIGHT).astype(jnp.int32)

    _sweep_key_blocks(qi, log_weights, accumulate, (ea_sc, ta_sc), (eb_sc, tb_sc),
                      more_to_come=any_weight_left)
    acc = acc_sc[...]
    lane = lax.broadcasted_iota(jnp.int32, (tq, LANES), 1)
    o_ref[...] = jnp.where(lane < LANES // 2, acc[:tq], acc[tq:]).astype(o_ref.dtype)


def _sb_prompt(q, kt, vt, *, tq, tk):
    b, s, w = q.shape
    groups = w // LANES
    assert tq & (tq - 1) == 0 and s % tq == 0 and tq == 2 * tk and tk % LANES == 0
    q_spec = pl.BlockSpec((None, tq, LANES), lambda bi, h, qi: (bi, qi, h))
    t_spec = pl.BlockSpec((None, LANES, s), lambda bi, h, qi: (bi, h, 0))
    block_bufs = [pltpu.VMEM((2 * tq, tk), F32), pltpu.VMEM((2 * tq, 1), F32)]
    return pl.pallas_call(
        functools.partial(_sb_prompt_kernel, tk=tk),
        grid=(b, groups, s // tq),
        in_specs=[q_spec, t_spec, t_spec],
        out_specs=q_spec,
        out_shape=jax.ShapeDtypeStruct((b, s, w), BF16),
        scratch_shapes=[pltpu.VMEM((2 * tq, 1), F32), pltpu.VMEM((2 * tq, LANES), F32)]
        + block_bufs + block_bufs,
        compiler_params=pltpu.CompilerParams(
            dimension_semantics=("parallel", "parallel", "arbitrary"),
            vmem_limit_bytes=VMEM_LIMIT_BYTES),
        name="sb_attn_prompt",
    )(q, kt, vt)


def _row_select(q, tok_of_row):
    out = jnp.zeros(tok_of_row.shape, q.dtype)
    for t in range(q.shape[0]):
        out = jnp.where(tok_of_row == t, jnp.broadcast_to(q[t:t + 1, :], out.shape), out)
    return out


def _new_token_mask(tok_of_row, n_tok, seq, strict):
    key = lax.broadcasted_iota(jnp.int32, tok_of_row.shape, 1)
    key_tok = key & (n_tok - 1)
    causal = key_tok < tok_of_row if strict else key_tok <= tok_of_row
    return ((key >> _log2(n_tok)) == seq) & causal


def _diff_sample_kernel(pt_ref, q_ref, knt_ref, vn_ref, lq1_ref, lk1_ref, lq2_ref, lk2_ref,
                        g_ref, *rest, n_pages_step, n_tok, lam_init):
    kt_refs = rest[:n_pages_step]
    v_refs = rest[n_pages_step:2 * n_pages_step]
    o_ref = rest[2 * n_pages_step]
    qbd_sc, m_sc, l_sc, acc_sc, s_sc = rest[2 * n_pages_step + 1:]
    b = pl.program_id(0)
    j = pl.program_id(1)
    rows, w = qbd_sc.shape
    page = kt_refs[0].shape[1]
    step_rows = n_pages_step * rows
    row_head = (lax.broadcasted_iota(jnp.int32, (rows, DA_V_DIM), 0) >> _log2(n_tok)) \
        & (DA_HEADS - 1)

    def scores(kts):
        qbd = qbd_sc[...]
        return [_dot(qbd, kt.astype(BF16)) for kt in kts]

    def attend(ss, v_heads):
        m_prev = m_sc[...]
        m_new = m_prev
        for s in ss:
            m_new = jnp.maximum(m_new, jnp.max(s, axis=-1, keepdims=True))
        alpha = jnp.exp(m_prev - m_new)
        l_new = alpha * l_sc[...]
        acc = alpha * acc_sc[...]
        for s, v_head in zip(ss, v_heads):
            p = jnp.exp(s - m_new)
            l_new = l_new + jnp.sum(p, axis=-1, keepdims=True)
            pb = p.astype(BF16)
            for h in range(DA_HEADS):
                pv = _dot(pb, v_head(h).astype(BF16))
                acc = acc + jnp.where(row_head == h, pv, 0.0)
        m_sc[...] = m_new
        l_sc[...] = l_new
        acc_sc[...] = acc

    @pl.when(j == 0)
    def _():
        row = lax.broadcasted_iota(jnp.int32, (rows, w), 0)
        lane = lax.broadcasted_iota(jnp.int32, (rows, w), 1)
        tok = row & (n_tok - 1)
        head_comp = ((row >> _log2(n_tok)) & (DA_HEADS - 1)) * 2 + (row >> _log2(n_tok * DA_HEADS))
        q_rows = _row_select(q_ref[...], tok)
        qbd = jnp.where((lane >> _log2(DA_HEAD_DIM)) == head_comp, q_rows, 0.0)
        qbd_sc[...] = qbd.astype(BF16)
        s_sc[pl.ds(step_rows, step_rows), :] = jnp.zeros((step_rows, page), F32)
        m_sc[...] = jnp.full(m_sc.shape, NEG_INF, F32)
        l_sc[...] = jnp.zeros(l_sc.shape, F32)
        acc_sc[...] = jnp.zeros(acc_sc.shape, F32)

    write_slot = pl.multiple_of((j & 1) * step_rows, step_rows)
    read_slot = pl.multiple_of(((j + 1) & 1) * step_rows, step_rows)
    prev = s_sc[pl.ds(read_slot, step_rows), :]
    s_sc[pl.ds(write_slot, step_rows), :] = jnp.concatenate(
        scores([r[...] for r in kt_refs]), axis=0)
    attend([prev[i * rows:(i + 1) * rows] for i in range(n_pages_step)],
           [lambda h, r=r: r[pl.ds(h, page, stride=DA_HEADS), :] for r in v_refs])

    @pl.when(j == 0)
    def _():
        m_sc[...] = jnp.full(m_sc.shape, NEG_INF, F32)
        l_sc[...] = jnp.zeros(l_sc.shape, F32)
        acc_sc[...] = jnp.zeros(acc_sc.shape, F32)
        n_new = knt_ref.shape[1]
        tok_n = lax.broadcasted_iota(jnp.int32, (rows, n_new), 0) & (n_tok - 1)
        mask = _new_token_mask(tok_n, n_tok, b, False)
        attend([jnp.where(mask, s, NEG_INF) for s in scores([knt_ref[...]])],
               [lambda h: vn_ref[:, h * DA_V_DIM:(h + 1) * DA_V_DIM]])

    @pl.when(j == pl.num_programs(1) - 1)
    def _():
        lam = _lam_from_refs(lq1_ref, lk1_ref, lq2_ref, lk2_ref, lam_init)
        o = acc_sc[...] / l_sc[...]
        o = o[:rows // 2] - lam * o[rows // 2:]
        o_ref[...] = _rmsnorm(o, g_ref[...]) * (1.0 - lam_init)


def _sb_sample_kernel(pt_ref, q_ref, knt_ref, vnt_ref, *rest, n_pages_step, n_tok):
    kt_refs = rest[:n_pages_step]
    vt_refs = rest[n_pages_step:2 * n_pages_step]
    o_ref = rest[2 * n_pages_step]
    qbd_sc, c_sc, acc_sc, e_sc, t_sc = rest[2 * n_pages_step + 1:]
    b = pl.program_id(0)
    j = pl.program_id(1)
    rows, w = qbd_sc.shape
    page = kt_refs[0].shape[1]
    step_rows = n_pages_step * rows
    u = _suffix_sum_matrix(page)

    def log_weights(kts, mask):
        qbd = qbd_sc[...]
        pairs = [_log2_keep(_dot(qbd, kt.astype(BF16))) for kt in kts]
        lk = jnp.concatenate([lk for lk, _ in pairs], axis=0)
        z2 = jnp.concatenate([z2 for _, z2 in pairs], axis=0)
        if mask is not None:
            lk = jnp.where(mask, lk, 0.0)
            z2 = jnp.where(mask, z2, NEG_INF)
        cs = _inclusive_suffix_sum(lk, u)
        return z2 + cs, cs[:, :1]

    def attend(e, totals, vts):
        c = c_sc[...]
        acc = acc_sc[...]
        for i, vt in enumerate(vts):
            a = jnp.exp2(e[i * rows:(i + 1) * rows] + c)
            acc = acc + _dot_nt(a.astype(BF16), vt.astype(BF16))
            c = c + totals[i * rows:(i + 1) * rows]
        c_sc[...] = c
        acc_sc[...] = acc

    @pl.when(j == 0)
    def _():
        row = lax.broadcasted_iota(jnp.int32, (rows, w), 0)
        lane = lax.broadcasted_iota(jnp.int32, (rows, w), 1)
        q_rows = _row_select(q_ref[...], row >> _log2(SB_HEADS))
        qbd = jnp.where((lane >> _log2(SB_HEAD_DIM)) == (row & (SB_HEADS - 1)), q_rows, 0.0)
        qbd_sc[...] = qbd.astype(BF16)
        e_sc[pl.ds(step_rows, step_rows), :] = jnp.zeros((step_rows, page), F32)
        t_sc[pl.ds(step_rows, step_rows), :] = jnp.zeros((step_rows, 1), F32)
        c_sc[...] = jnp.zeros(c_sc.shape, F32)
        acc_sc[...] = jnp.zeros(acc_sc.shape, F32)

    write_slot = pl.multiple_of((j & 1) * step_rows, step_rows)
    read_slot = pl.multiple_of(((j + 1) & 1) * step_rows, step_rows)
    e_prev = e_sc[pl.ds(read_slot, step_rows), :]
    t_prev = t_sc[pl.ds(read_slot, step_rows), :]
    e_new, t_new = log_weights([r[...] for r in kt_refs], None)
    e_sc[pl.ds(write_slot, step_rows), :] = e_new
    t_sc[pl.ds(write_slot, step_rows), :] = t_new
    attend(e_prev, t_prev, [r[...] for r in vt_refs])

    @pl.when(j == 0)
    def _():
        c_sc[...] = jnp.zeros(c_sc.shape, F32)
        acc_sc[...] = jnp.zeros(acc_sc.shape, F32)
        n_new = knt_ref.shape[1]
        assert n_new == page
        tok_n = lax.broadcasted_iota(jnp.int32, (rows, n_new), 0) >> _log2(SB_HEADS)
        mask = _new_token_mask(tok_n, n_tok, b, True)
        attend(*log_weights([knt_ref[...]], mask), [vnt_ref[...]])

    @pl.when(j == pl.num_programs(1) - 1)
    def _():
        acc = acc_sc[...]
        h = lax.broadcasted_iota(jnp.int32, (SB_HEADS, w), 0)
        lane = lax.broadcasted_iota(jnp.int32, (SB_HEADS, w), 1)
        keep = (lane >> _log2(SB_HEAD_DIM)) == h
        for t in range(n_tok):
            tile = acc[t * SB_HEADS:(t + 1) * SB_HEADS]
            o_ref[t:t + 1, :] = jnp.sum(jnp.where(keep, tile, 0.0), axis=0, keepdims=True)


def _diff_sample_scratch(n_tok, page, per_step):
    rows = 2 * DA_HEADS * n_tok
    return [pltpu.VMEM((rows, DA_HEADS * 2 * DA_HEAD_DIM), BF16),
            pltpu.VMEM((rows, 1), F32), pltpu.VMEM((rows, 1), F32),
            pltpu.VMEM((rows, DA_V_DIM), F32),
            pltpu.VMEM((2 * per_step * rows, page), F32)]


def _sb_sample_scratch(n_tok, page, per_step):
    rows = SB_HEADS * n_tok
    w = SB_HEADS * SB_HEAD_DIM
    return [pltpu.VMEM((rows, w), BF16),
            pltpu.VMEM((rows, 1), F32), pltpu.VMEM((rows, w), F32),
            pltpu.VMEM((2 * per_step * rows, page), F32),
            pltpu.VMEM((2 * per_step * rows, 1), F32)]


def _page_index(b, j, pt_ref, *, slot, per_step, n_pages, reverse, lag):
    step = jnp.clip(j - lag, 0, n_pages // per_step - 1)
    p = step * per_step + slot
    if reverse:
        p = n_pages - 1 - p
    return (pt_ref[b, p], 0, 0)


def _sample_attention(kern, q, new_k, new_v, extras, cache_k, cache_v, page_table, *,
                      reverse, out_block, scratch, name):
    n_seq, n_tok, w = q.shape
    n_pages = page_table.shape[1]
    per_step = PAGES_PER_STEP
    assert n_pages % per_step == 0
    seq_spec = pl.BlockSpec((None, n_tok, w), lambda b, j, pt: (b, 0, 0))
    out_spec = pl.BlockSpec((None, *out_block), lambda b, j, pt: (b, 0, 0))

    def page_specs(cache, lag):
        return [pl.BlockSpec((None, *cache.shape[1:]),
                             functools.partial(_page_index, slot=i, per_step=per_step,
                                               n_pages=n_pages, reverse=reverse, lag=lag))
                for i in range(per_step)]

    grid_spec = pltpu.PrefetchScalarGridSpec(
        num_scalar_prefetch=1,
        grid=(n_seq, n_pages // per_step + 1),
        in_specs=[seq_spec] + [_const_spec(e.shape) for e in (new_k, new_v, *extras)]
        + page_specs(cache_k, 0) + page_specs(cache_v, 1),
        out_specs=out_spec,
        scratch_shapes=scratch(per_step),
    )
    return pl.pallas_call(
        functools.partial(kern, n_pages_step=per_step, n_tok=n_tok),
        grid_spec=grid_spec,
        out_shape=jax.ShapeDtypeStruct((n_seq, *out_block), F32),
        compiler_params=pltpu.CompilerParams(
            dimension_semantics=("parallel", "arbitrary"),
            vmem_limit_bytes=VMEM_LIMIT_BYTES),
        name=name,
    )(page_table, q, new_k, new_v, *extras, *([cache_k] * per_step), *([cache_v] * per_step))


def _rope_tables(pos):
    half = ROPE_DIM // 2
    inv_freq = ROPE_THETA ** (-jnp.arange(0, ROPE_DIM, 2, dtype=F32) / ROPE_DIM)
    ang = pos.astype(F32)[:, None] * inv_freq[None, :]
    cos, sin = jnp.cos(ang), jnp.sin(ang)
    n = pos.shape[0]
    rest = DA_HEAD_DIM - ROPE_DIM
    zeros_h = jnp.zeros((n, half), F32)
    cos_r = jnp.concatenate([cos, cos, jnp.ones((n, rest), F32)], axis=1)
    sin_next = jnp.concatenate([-sin, zeros_h, jnp.zeros((n, rest), F32)], axis=1)
    sin_prev = jnp.concatenate([zeros_h, sin, jnp.zeros((n, rest), F32)], axis=1)
    reps = LANES // DA_HEAD_DIM
    return (tuple(jnp.tile(t, (1, reps)) for t in (cos_r, sin_next, sin_prev)),
            (cos.T, sin.T))


def kernel(x_prompt, x_sample, cache_diff_k, cache_diff_v, cache_sb_k, cache_sb_v, page_table,
           norm1, w_ffn1_gate, w_ffn1_up, w_ffn1_down, norm2, w_in,
           lambda_q1, lambda_k1, lambda_q2, lambda_k2, subln_g,
           w_branch_a, w_branch_b, w_out, norm3, w_ffn2_gate, w_ffn2_up, w_ffn2_down, norm_f):
    bsz, seq, d = x_prompt.shape
    n_seq, n_tok, _ = x_sample.shape
    depth = norm1.shape[0]
    assert depth == 1, "the final norm is fused into the last layer's second FFN"
    pool, page = cache_diff_k.shape[1:3]
    past_len = page_table.shape[1] * page
    qk_w = DA_HEADS * 2 * DA_HEAD_DIM
    v_w = DA_HEADS * DA_V_DIM
    sb_w = SB_HEADS * SB_HEAD_DIM
    n_s = n_seq * n_tok

    xp = x_prompt.reshape(bsz * seq, d)
    xs = x_sample.reshape(n_s, d)
    tabs_p = _rope_tables(jnp.arange(seq))
    tabs_s = _rope_tables(jnp.tile(past_len + jnp.arange(n_tok), n_seq))
    gf = norm_f.reshape(1, d)
    new_kv_p, new_kv_s = [], []

    for l in range(depth):
        lam_init = 0.8 - 0.6 * math.exp(-0.3 * l)
        row = lambda a: a[l][None]
        cast = lambda a: a[l].astype(BF16)
        lams = tuple(row(a) for a in (lambda_q1, lambda_k1, lambda_q2, lambda_k2))
        ffn1 = (row(norm1), cast(w_ffn1_gate), cast(w_ffn1_up), cast(w_ffn1_down))
        merge = (cast(w_branch_a), cast(w_branch_b), cast(w_out), row(norm3),
                 cast(w_ffn2_gate), cast(w_ffn2_up), cast(w_ffn2_down), gf)
        w_in_l = cast(w_in)
        w_t = jnp.concatenate(
            [w_in_l[:, o:o + qk_w].T for o in (qk_w, 2 * qk_w + v_w + sb_w,
                                               2 * qk_w + v_w + 2 * sb_w)], axis=0)
        g_sub = row(subln_g)

        xp = _ffn_half(xp, *ffn1, tm=FFN_TOKEN_TILE)
        (qa, qb, ka_t, va, kb_t, vb_t, ga, gb, ka_h, va_h, kb_h, vb_h) = _proj(
            xp, row(norm2), w_in_l, w_t, *tabs_p, tm=PROJ_TOKEN_TILE, sample=False)
        shp = lambda a: a.reshape(bsz, seq, a.shape[-1])
        oa = _diff_prompt(shp(qa), ka_h, shp(va_h), lams, g_sub, lam_init=lam_init,
                          tq=ATTN_QUERY_TILE, tk=ATTN_KEY_BLOCK)
        ob = _sb_prompt(shp(qb), kb_h, vb_h, tq=ATTN_QUERY_TILE, tk=ATTN_KEY_BLOCK)
        xp = _merge_ffn(xp, oa.reshape(bsz * seq, v_w), ob.reshape(bsz * seq, sb_w), ga, gb,
                        *merge, tm=FFN_TOKEN_TILE)
        new_kv_p.append((
            jnp.transpose(ka_t.reshape(bsz, DA_HEADS, 2, DA_HEAD_DIM, seq), (0, 4, 1, 2, 3)),
            va.reshape(bsz, seq, DA_HEADS, DA_V_DIM),
            jnp.transpose(kb_t.reshape(bsz, SB_HEADS, SB_HEAD_DIM, seq), (0, 3, 1, 2)),
            jnp.transpose(vb_t.reshape(bsz, SB_HEADS, SB_HEAD_DIM, seq), (0, 3, 1, 2))))

        xs = _ffn_half(xs, *ffn1, tm=n_s)
        (qa, qb, ka_t, va, kb_t, vb_t, ga, gb, ka, kb, vb) = _proj(
            xs, row(norm2), w_in_l, w_t, *tabs_s, tm=n_s, sample=True)
        shs = lambda a: a.reshape(n_seq, n_tok, a.shape[-1])
        ck_a = jnp.transpose(cache_diff_k[l], (0, 2, 3, 4, 1)).reshape(pool, qk_w, page)
        cv_a = cache_diff_v[l].reshape(pool, page * DA_HEADS, DA_V_DIM)
        ck_b = jnp.transpose(cache_sb_k[l], (0, 2, 3, 1)).reshape(pool, sb_w, page)
        cv_b = jnp.transpose(cache_sb_v[l], (0, 2, 3, 1)).reshape(pool, sb_w, page)
        oa = _sample_attention(
            functools.partial(_diff_sample_kernel, lam_init=lam_init),
            shs(qa), ka_t[0], va, (*lams, g_sub), ck_a, cv_a, page_table,
            reverse=False, out_block=(DA_HEADS * n_tok, DA_V_DIM),
            scratch=functools.partial(_diff_sample_scratch, n_tok, page),
            name="diff_attn_sample")
        oa = jnp.transpose(oa.reshape(n_seq, DA_HEADS, n_tok, DA_V_DIM), (0, 2, 1, 3))
        ob = _sample_attention(
            _sb_sample_kernel, shs(qb), kb_t[0], vb_t[0], (), ck_b, cv_b, page_table,
            reverse=True, out_block=(n_tok, sb_w),
            scratch=functools.partial(_sb_sample_scratch, n_tok, page),
            name="sb_attn_sample")
        xs = _merge_ffn(xs, oa.reshape(n_s, v_w), ob.reshape(n_s, sb_w), ga, gb, *merge, tm=n_s)
        new_kv_s.append((ka.reshape(n_seq, n_tok, DA_HEADS, 2, DA_HEAD_DIM),
                         va.reshape(n_seq, n_tok, DA_HEADS, DA_V_DIM),
                         kb.reshape(n_seq, n_tok, SB_HEADS, SB_HEAD_DIM),
                         vb.reshape(n_seq, n_tok, SB_HEADS, SB_HEAD_DIM)))

    stack = lambda items, i: jnp.stack([it[i] for it in items])
    return (xp.reshape(bsz, seq, d), xs.reshape(n_seq, n_tok, d),
            *(stack(new_kv_p, i) for i in range(4)),
            *(stack(new_kv_s, i) for i in range(4)))
```

```python
import functools
import math

import jax
import jax.numpy as jnp
from jax import lax
from jax.experimental import pallas as pl
from jax.experimental.pallas import tpu as pltpu

F32 = jnp.float32
BF16 = jnp.bfloat16

DA_HEADS = 4
DA_HEAD_DIM = 64
DA_V_DIM = 2 * DA_HEAD_DIM
SB_HEADS = 8
SB_HEAD_DIM = 64
ROPE_THETA = 500000.0
ROPE_DIM = DA_HEAD_DIM // 4
FFN_RES = 0.5
RMS_EPS = 1e-6
NEG_INF = -1e30

LANES = 128
SUBLANES = 8
VMEM_LIMIT_BYTES = 56 * 1024 * 1024

FFN_TOKEN_TILE = 512
FFN_CHUNK = 1408
PROJ_TOKEN_TILE = 256
ATTN_QUERY_TILE = 512
ATTN_KEY_BLOCK = 256
PAGES_PER_STEP = 32
SB_PAGE_GROUP = 4


def _const_spec(shape):
    zeros = (0,) * len(shape)
    return pl.BlockSpec(shape, lambda *_: zeros, pipeline_mode=pl.Buffered(1))


def _log2(n):
    assert n & (n - 1) == 0
    return n.bit_length() - 1


def _rmsnorm(x, g):
    ms = jnp.mean(x * x, axis=-1, keepdims=True)
    return x * lax.rsqrt(ms + RMS_EPS) * g


def _dot(a, b):
    return jnp.dot(a, b, preferred_element_type=F32)


def _dot_nt(a, b):
    return lax.dot_general(a, b, (((1,), (1,)), ((), ())), preferred_element_type=F32)


def _swiglu(xn, wg_ref, wu_ref, wd_ref):
    xb = xn.astype(BF16)
    d_ff = wg_ref.shape[1]
    chunk = FFN_CHUNK if d_ff % FFN_CHUNK == 0 else d_ff
    acc = None
    for c in range(d_ff // chunk):
        lo, hi = c * chunk, (c + 1) * chunk
        hg = _dot(xb, wg_ref[:, lo:hi])
        hu = _dot(xb, wu_ref[:, lo:hi])
        h = (hg * jax.nn.sigmoid(hg) * hu).astype(BF16)
        part = _dot(h, wd_ref[lo:hi, :])
        acc = part if acc is None else acc + part
    return acc


def _lam_from_refs(lq1_ref, lk1_ref, lq2_ref, lk2_ref, lam_init):
    a1 = jnp.sum(lq1_ref[...] * lk1_ref[...], axis=-1, keepdims=True)
    a2 = jnp.sum(lq2_ref[...] * lk2_ref[...], axis=-1, keepdims=True)
    return jnp.exp(a1) - jnp.exp(a2) + lam_init


LOG2_E = 1.4426950408889634
DEAD_LOG2_WEIGHT = -150.0


def _log2_keep(z):
    z2 = z * LOG2_E
    nz2 = z * (-LOG2_E)
    return jnp.minimum(nz2, 0.0) - jnp.log2(1.0 + jnp.exp2(jnp.minimum(z2, nz2))), z2


def _suffix_sum_matrix(n):
    j = lax.broadcasted_iota(jnp.int32, (2 * n, n), 0) & (n - 1)
    k = lax.broadcasted_iota(jnp.int32, (2 * n, n), 1)
    return jnp.where(j >= k, 1.0, 0.0).astype(BF16)


def _inclusive_suffix_sum(x, u2):
    hi = x.astype(BF16)
    lo = (x - hi.astype(F32)).astype(BF16)
    return _dot(jnp.concatenate([hi, lo], axis=1), u2)


def _ffn_half_kernel(x_ref, g_ref, wg_ref, wu_ref, wd_ref, o_ref):
    x = x_ref[...]
    o_ref[...] = x + FFN_RES * _swiglu(_rmsnorm(x, g_ref[...]), wg_ref, wu_ref, wd_ref)


def _ffn_half(x, g, wg, wu, wd, *, tm):
    n, d = x.shape
    d_ff = wg.shape[1]
    row = pl.BlockSpec((tm, d), lambda i: (i, 0))
    return pl.pallas_call(
        _ffn_half_kernel,
        grid=(n // tm,),
        in_specs=[row, _const_spec((1, d)), _const_spec((d, d_ff)), _const_spec((d, d_ff)),
                  _const_spec((d_ff, d))],
        out_specs=row,
        out_shape=jax.ShapeDtypeStruct((n, d), F32),
        compiler_params=pltpu.CompilerParams(
            dimension_semantics=("parallel",), vmem_limit_bytes=VMEM_LIMIT_BYTES),
        name="ffn_half",
    )(x, g, wg, wu, wd)


def _rope_rows(t, cos, sin_next, sin_prev):
    half = ROPE_DIM // 2
    groups = []
    for c in range(0, t.shape[1], LANES):
        x = t[:, c:c + LANES]
        nxt = pltpu.roll(x, LANES - half, axis=1)
        prv = pltpu.roll(x, half, axis=1)
        groups.append(x * cos + nxt * sin_next + prv * sin_prev)
    return jnp.concatenate(groups, axis=1)


def _rope_cols(t, cos_t, sin_t):
    half = ROPE_DIM // 2
    assert half == SUBLANES
    pieces = []
    for r in range(0, t.shape[0], DA_HEAD_DIM):
        x1 = t[r:r + half]
        x2 = t[r + half:r + 2 * half]
        pieces += [x1 * cos_t - x2 * sin_t, x2 * cos_t + x1 * sin_t,
                   t[r + 2 * half:r + DA_HEAD_DIM]]
    return jnp.concatenate(pieces, axis=0)


_QA, _KA, _VA, _QB, _KB, _VB, _GA, _GB = range(8)


def _proj_kernel(x_ref, g_ref, w_ref, wt_ref, cos_ref, sn_ref, sp_ref, cos_t_ref, sin_t_ref,
                 *outs, offs, da_scale, sb_scale, sample):
    xb = _rmsnorm(x_ref[...], g_ref[...]).astype(BF16)

    def seg(i):
        return _dot(xb, w_ref[:, offs[i]:offs[i + 1]])

    def seg_t(i):
        w = offs[_KA + 1] - offs[_KA]
        return _dot_nt(wt_ref[i * w:(i + 1) * w, :], xb)

    qa_o, qb_o, ka_t, va_o, kb_t, vb_t, ga_o, gb_o = outs[:8]
    rope_tabs = (cos_ref[...], sn_ref[...], sp_ref[...])
    qa_o[...] = (_rope_rows(seg(_QA), *rope_tabs) * da_scale).astype(qa_o.dtype)
    qb_o[...] = (seg(_QB) * sb_scale).astype(qb_o.dtype)
    ka = _rope_cols(seg_t(0), cos_t_ref[...], sin_t_ref[...])
    kb = seg_t(1)
    vb = seg_t(2)
    va = seg(_VA)
    ka_t[...] = ka
    va_o[...] = va
    kb_t[...] = kb
    vb_t[...] = vb
    ga_o[...] = seg(_GA)
    gb_o[...] = seg(_GB)
    if sample:
        ka_o, kb_o, vb_o = outs[8:]
        ka_o[...] = _rope_rows(seg(_KA), *rope_tabs)
        kb_o[...] = seg(_KB)
        vb_o[...] = seg(_VB)
    else:
        ka_h, va_h, kb_h, vb_h = outs[8:]
        ka_h[...] = ka.astype(BF16)
        va_h[...] = va.astype(BF16)
        kb_h[...] = kb.astype(BF16)
        vb_h[...] = vb.astype(BF16)


def _proj(x, g, w_in, w_t, tabs_rows, tabs_cols, *, tm, sample):
    n, d = x.shape
    qk_w = DA_HEADS * 2 * DA_HEAD_DIM
    v_w = DA_HEADS * DA_V_DIM
    sb_w = SB_HEADS * SB_HEAD_DIM
    assert qk_w == sb_w and w_t.shape[0] == 3 * qk_w
    widths = (qk_w, qk_w, v_w, sb_w, sb_w, sb_w, d, d)
    assert sum(widths) == w_in.shape[1]
    offs = tuple(sum(widths[:i]) for i in range(len(widths) + 1))
    period = tabs_rows[0].shape[0]
    pos_blocks = period // tm
    n_batch = n // period
    q_dtype = F32 if sample else BF16

    def row(w):
        return pl.BlockSpec((tm, w), lambda i: (i, 0))

    col = pl.BlockSpec((None, qk_w, tm), lambda i: (i // pos_blocks, 0, i % pos_blocks))
    tab_r = pl.BlockSpec((tm, LANES), lambda i: (i % pos_blocks, 0))
    tab_c = pl.BlockSpec((ROPE_DIM // 2, tm), lambda i: (0, i % pos_blocks))

    def rows(w, dt):
        return jax.ShapeDtypeStruct((n, w), dt), row(w)

    def cols(dt):
        return jax.ShapeDtypeStruct((n_batch, qk_w, period), dt), col

    outs = [rows(qk_w, q_dtype), rows(sb_w, q_dtype), cols(F32), rows(v_w, F32), cols(F32),
            cols(F32), rows(d, F32), rows(d, F32)]
    if sample:
        outs += [rows(qk_w, F32), rows(sb_w, F32), rows(sb_w, F32)]
    else:
        outs += [cols(BF16), rows(v_w, BF16), cols(BF16), cols(BF16)]
    kern = functools.partial(_proj_kernel, offs=offs, da_scale=DA_HEAD_DIM ** -0.5,
                             sb_scale=SB_HEAD_DIM ** -0.5, sample=sample)
    return pl.pallas_call(
        kern,
        grid=(n // tm,),
        in_specs=[row(d), _const_spec((1, d)), _const_spec(w_in.shape), _const_spec(w_t.shape),
                  tab_r, tab_r, tab_r, tab_c, tab_c],
        out_specs=[o[1] for o in outs],
        out_shape=[o[0] for o in outs],
        compiler_params=pltpu.CompilerParams(
            dimension_semantics=("parallel",), vmem_limit_bytes=VMEM_LIMIT_BYTES),
        name="in_proj",
    )(x, g, w_in, w_t, *tabs_rows, *tabs_cols)


def _merge_ffn_kernel(x_ref, oa_ref, ob_ref, ga_ref, gb_ref, wa_ref, wb_ref, wo_ref,
                      g3_ref, wg_ref, wu_ref, wd_ref, gf_ref, o_ref):
    ya = _dot(oa_ref[...].astype(BF16), wa_ref[...])
    yb = _dot(ob_ref[...].astype(BF16), wb_ref[...])
    m = jax.nn.sigmoid(ga_ref[...]) * ya + jax.nn.sigmoid(gb_ref[...]) * yb
    x = x_ref[...] + _dot(m.astype(BF16), wo_ref[...])
    y = x + FFN_RES * _swiglu(_rmsnorm(x, g3_ref[...]), wg_ref, wu_ref, wd_ref)
    o_ref[...] = _rmsnorm(y, gf_ref[...])


def _merge_ffn(x, oa, ob, ga, gb, wa, wb, wo, g3, wg, wu, wd, gf, *, tm):
    n, d = x.shape

    def row(w):
        return pl.BlockSpec((tm, w), lambda i: (i, 0))

    consts = [wa, wb, wo, g3, wg, wu, wd, gf]
    return pl.pallas_call(
        _merge_ffn_kernel,
        grid=(n // tm,),
        in_specs=[row(d), row(oa.shape[1]), row(ob.shape[1]), row(d), row(d)]
        + [_const_spec(c.shape) for c in consts],
        out_specs=row(d),
        out_shape=jax.ShapeDtypeStruct((n, d), F32),
        compiler_params=pltpu.CompilerParams(
            dimension_semantics=("parallel",), vmem_limit_bytes=VMEM_LIMIT_BYTES),
        name="merge_ffn",
    )(x, oa, ob, ga, gb, *consts)


def _stack_halves(q):
    lane = lax.broadcasted_iota(jnp.int32, q.shape, 1)
    zero = jnp.zeros_like(q)
    return jnp.concatenate([jnp.where(lane < LANES // 2, q, zero),
                            jnp.where(lane >= LANES // 2, q, zero)], axis=0)


def _causal_keep(shape, tq, key_offset, strict):
    row = lax.broadcasted_iota(jnp.int32, shape, 0) & (tq - 1)
    key = lax.broadcasted_iota(jnp.int32, shape, 1) + key_offset
    return key < row if strict else key <= row


def _lane_chunks(x):
    return [x[:, c:c + LANES] for c in range(0, x.shape[1], LANES)]


def _sweep_key_blocks(qi, produce, consume, bufs_a, bufs_b, more_to_come=None):
    def put(bufs, vals):
        for buf, val in zip(bufs, vals):
            buf[...] = val

    def get(bufs):
        return [buf[...] for buf in bufs]

    n_full = 2 * qi
    put(bufs_b, produce(n_full + 1, 1))
    put(bufs_a, produce(n_full, 0))
    consume(n_full + 1, *get(bufs_b))

    def pair(i):
        kb = n_full - 1 - 2 * i
        put(bufs_b, produce(kb, None))
        consume(kb + 1, *get(bufs_a))
        put(bufs_a, produce(kb - 1, None))
        consume(kb, *get(bufs_b))

    if more_to_come is None:
        def body(i, carry):
            pair(i)
            return carry

        pairs_done = lax.fori_loop(0, qi, body, qi)
    else:
        def cond(state):
            i, go_on = state
            return jnp.logical_and(i < qi, go_on > 0)

        def body(state):
            i, _ = state
            pair(i)
            return i + 1, more_to_come()

        pairs_done, _ = lax.while_loop(cond, body, (jnp.int32(0), more_to_come()))
    consume(n_full - 2 * pairs_done, *get(bufs_a))


def _diff_prompt_kernel(q_ref, kt_ref, v_ref, lq1_ref, lk1_ref, lq2_ref, lk2_ref, g_ref,
                        o_ref, mx_sc, l_sc, acc_sc, sa_sc, sb_sc, *, lam_init, tk):
    qi = pl.program_id(2)
    tq = q_ref.shape[0]
    qq = _stack_halves(q_ref[...])

    def scores(kb, causal_block):
        start = pl.multiple_of(kb * tk, tk)
        s = _dot(qq, kt_ref[:, pl.ds(start, tk)])
        if causal_block is not None:
            keep = _causal_keep(s.shape, tq, causal_block * tk, False)
            s = jnp.where(keep, s, NEG_INF)
        return (s,)

    mx_sc[...] = jnp.full(mx_sc.shape, NEG_INF, F32)

    def fold_max(kb, s):
        mx = mx_sc[...]
        for chunk in _lane_chunks(s):
            mx = jnp.maximum(mx, chunk)
        mx_sc[...] = mx

    _sweep_key_blocks(qi, scores, fold_max, (sa_sc,), (sb_sc,))
    m = jnp.max(mx_sc[...], axis=-1, keepdims=True)
    mx_sc[...] = jnp.broadcast_to(m, mx_sc.shape)

    l_sc[...] = jnp.zeros(l_sc.shape, F32)
    acc_sc[...] = jnp.zeros(acc_sc.shape, F32)

    def accumulate(kb, s):
        start = pl.multiple_of(kb * tk, tk)
        m_rep = mx_sc[...]
        ps = [jnp.exp(chunk - m_rep) for chunk in _lane_chunks(s)]
        l_new = l_sc[...]
        for p in ps:
            l_new = l_new + p
        l_sc[...] = l_new
        p = jnp.concatenate(ps, axis=1).astype(BF16)
        acc_sc[...] += _dot(p, v_ref[pl.ds(start, tk), :])

    _sweep_key_blocks(qi, scores, accumulate, (sa_sc,), (sb_sc,))

    lam = _lam_from_refs(lq1_ref, lk1_ref, lq2_ref, lk2_ref, lam_init)
    o = acc_sc[...] / jnp.sum(l_sc[...], axis=-1, keepdims=True)
    o = o[:tq] - lam * o[tq:]
    o_ref[...] = (_rmsnorm(o, g_ref[...]) * (1.0 - lam_init)).astype(o_ref.dtype)


def _diff_prompt(q, kt, v, lams, g, *, lam_init, tq, tk):
    b, s, w = q.shape
    heads = w // LANES
    assert tq & (tq - 1) == 0 and s % tq == 0 and tq == 2 * tk and tk % LANES == 0
    q_spec = pl.BlockSpec((None, tq, LANES), lambda bi, h, qi: (bi, qi, h))
    kt_spec = pl.BlockSpec((None, LANES, s), lambda bi, h, qi: (bi, h, 0))
    v_spec = pl.BlockSpec((None, s, LANES), lambda bi, h, qi: (bi, 0, h))
    small = [_const_spec(a.shape) for a in (*lams, g)]
    return pl.pallas_call(
        functools.partial(_diff_prompt_kernel, lam_init=lam_init, tk=tk),
        grid=(b, heads, s // tq),
        in_specs=[q_spec, kt_spec, v_spec] + small,
        out_specs=q_spec,
        out_shape=jax.ShapeDtypeStruct((b, s, w), BF16),
        scratch_shapes=[pltpu.VMEM((2 * tq, LANES), F32)] * 3
        + [pltpu.VMEM((2 * tq, tk), F32)] * 2,
        compiler_params=pltpu.CompilerParams(
            dimension_semantics=("parallel", "parallel", "arbitrary"),
            vmem_limit_bytes=VMEM_LIMIT_BYTES),
        name="diff_attn_prompt",
    )(q, kt, v, *lams, g)


def _sb_prompt_kernel(q_ref, kt_ref, vt_ref, o_ref, c_sc, acc_sc, ea_sc, ta_sc, eb_sc, tb_sc,
                      *, tk):
    qi = pl.program_id(2)
    tq = q_ref.shape[0]
    qq = _stack_halves(q_ref[...])
    u = _suffix_sum_matrix(tk)
    c_sc[...] = jnp.zeros(c_sc.shape, F32)
    acc_sc[...] = jnp.zeros(acc_sc.shape, F32)

    def log_weights(kb, causal_block):
        start = pl.multiple_of(kb * tk, tk)
        z = _dot(qq, kt_ref[:, pl.ds(start, tk)])
        lk, z2 = _log2_keep(z)
        if causal_block is not None:
            keep = _causal_keep(z.shape, tq, causal_block * tk, True)
            lk = jnp.where(keep, lk, 0.0)
            z2 = jnp.where(keep, z2, NEG_INF)
        cs = _inclusive_suffix_sum(lk, u)
        return z2 + cs, cs[:, :1]

    def accumulate(kb, e, total):
        start = pl.multiple_of(kb * tk, tk)
        a = jnp.exp2(e + c_sc[...])
        acc_sc[...] += _dot_nt(a.astype(BF16), vt_ref[:, pl.ds(start, tk)])
        c_sc[...] += total

    def any_weight_left():
        return (jnp.max(c_sc[...]) > DEAD_LOG2_WEIGHT).astype(jnp.int32)

    _sweep_key_blocks(qi, log_weights, accumulate, (ea_sc, ta_sc), (eb_sc, tb_sc),
                      more_to_come=any_weight_left)
    acc = acc_sc[...]
    lane = lax.broadcasted_iota(jnp.int32, (tq, LANES), 1)
    o_ref[...] = jnp.where(lane < LANES // 2, acc[:tq], acc[tq:]).astype(o_ref.dtype)


def _sb_prompt(q, kt, vt, *, tq, tk):
    b, s, w = q.shape
    groups = w // LANES
    assert tq & (tq - 1) == 0 and s % tq == 0 and tq == 2 * tk and tk % LANES == 0
    q_spec = pl.BlockSpec((None, tq, LANES), lambda bi, h, qi: (bi, qi, h))
    t_spec = pl.BlockSpec((None, LANES, s), lambda bi, h, qi: (bi, h, 0))
    block_bufs = [pltpu.VMEM((2 * tq, tk), F32), pltpu.VMEM((2 * tq, 1), F32)]
    return pl.pallas_call(
        functools.partial(_sb_prompt_kernel, tk=tk),
        grid=(b, groups, s // tq),
        in_specs=[q_spec, t_spec, t_spec],
        out_specs=q_spec,
        out_shape=jax.ShapeDtypeStruct((b, s, w), BF16),
        scratch_shapes=[pltpu.VMEM((2 * tq, 1), F32), pltpu.VMEM((2 * tq, LANES), F32)]
        + block_bufs + block_bufs,
        compiler_params=pltpu.CompilerParams(
            dimension_semantics=("parallel", "parallel", "arbitrary"),
            vmem_limit_bytes=VMEM_LIMIT_BYTES),
        name="sb_attn_prompt",
    )(q, kt, vt)


def _row_select(q, tok_of_row):
    out = jnp.zeros(tok_of_row.shape, q.dtype)
    for t in range(q.shape[0]):
        out = jnp.where(tok_of_row == t, jnp.broadcast_to(q[t:t + 1, :], out.shape), out)
    return out


def _new_token_mask(tok_of_row, n_tok, seq, strict):
    key = lax.broadcasted_iota(jnp.int32, tok_of_row.shape, 1)
    key_tok = key & (n_tok - 1)
    causal = key_tok < tok_of_row if strict else key_tok <= tok_of_row
    return ((key >> _log2(n_tok)) == seq) & causal


def _diff_sample_kernel(pt_ref, q_ref, knt_ref, vn_ref, lq1_ref, lk1_ref, lq2_ref, lk2_ref,
                        g_ref, *rest, n_pages_step, n_tok, lam_init):
    kt_refs = rest[:n_pages_step]
    v_refs = rest[n_pages_step:2 * n_pages_step]
    o_ref = rest[2 * n_pages_step]
    qbd_sc, m_sc, l_sc, acc_sc, s_sc = rest[2 * n_pages_step + 1:]
    b = pl.program_id(0)
    j = pl.program_id(1)
    rows, w = qbd_sc.shape
    page = kt_refs[0].shape[1]
    step_rows = n_pages_step * rows
    row_head = (lax.broadcasted_iota(jnp.int32, (rows, DA_V_DIM), 0) >> _log2(n_tok)) \
        & (DA_HEADS - 1)

    def scores(kts):
        qbd = qbd_sc[...]
        return [_dot(qbd, kt.astype(BF16)) for kt in kts]

    def attend(ss, v_heads):
        m_prev = m_sc[...]
        m_new = m_prev
        for s in ss:
            m_new = jnp.maximum(m_new, jnp.max(s, axis=-1, keepdims=True))
        alpha = jnp.exp(m_prev - m_new)
        l_new = alpha * l_sc[...]
        acc = alpha * acc_sc[...]
        for s, v_head in zip(ss, v_heads):
            p = jnp.exp(s - m_new)
            l_new = l_new + jnp.sum(p, axis=-1, keepdims=True)
            pb = p.astype(BF16)
            for h in range(DA_HEADS):
                pv = _dot(pb, v_head(h).astype(BF16))
                acc = acc + jnp.where(row_head == h, pv, 0.0)
        m_sc[...] = m_new
        l_sc[...] = l_new
        acc_sc[...] = acc

    @pl.when(j == 0)
    def _():
        row = lax.broadcasted_iota(jnp.int32, (rows, w), 0)
        lane = lax.broadcasted_iota(jnp.int32, (rows, w), 1)
        tok = row & (n_tok - 1)
        head_comp = ((row >> _log2(n_tok)) & (DA_HEADS - 1)) * 2 + (row >> _log2(n_tok * DA_HEADS))
        q_rows = _row_select(q_ref[...], tok)
        qbd = jnp.where((lane >> _log2(DA_HEAD_DIM)) == head_comp, q_rows, 0.0)
        qbd_sc[...] = qbd.astype(BF16)
        s_sc[pl.ds(step_rows, step_rows), :] = jnp.zeros((step_rows, page), F32)
        m_sc[...] = jnp.full(m_sc.shape, NEG_INF, F32)
        l_sc[...] = jnp.zeros(l_sc.shape, F32)
        acc_sc[...] = jnp.zeros(acc_sc.shape, F32)

    write_slot = pl.multiple_of((j & 1) * step_rows, step_rows)
    read_slot = pl.multiple_of(((j + 1) & 1) * step_rows, step_rows)
    prev = s_sc[pl.ds(read_slot, step_rows), :]
    s_sc[pl.ds(write_slot, step_rows), :] = jnp.concatenate(
        scores([r[...] for r in kt_refs]), axis=0)
    attend([prev[i * rows:(i + 1) * rows] for i in range(n_pages_step)],
           [lambda h, r=r: r[pl.ds(h, page, stride=DA_HEADS), :] for r in v_refs])

    @pl.when(j == 0)
    def _():
        m_sc[...] = jnp.full(m_sc.shape, NEG_INF, F32)
        l_sc[...] = jnp.zeros(l_sc.shape, F32)
        acc_sc[...] = jnp.zeros(acc_sc.shape, F32)
        n_new = knt_ref.shape[1]
        tok_n = lax.broadcasted_iota(jnp.int32, (rows, n_new), 0) & (n_tok - 1)
        mask = _new_token_mask(tok_n, n_tok, b, False)
        attend([jnp.where(mask, s, NEG_INF) for s in scores([knt_ref[...]])],
               [lambda h: vn_ref[:, h * DA_V_DIM:(h + 1) * DA_V_DIM]])

    @pl.when(j == pl.num_programs(1) - 1)
    def _():
        lam = _lam_from_refs(lq1_ref, lk1_ref, lq2_ref, lk2_ref, lam_init)
        o = acc_sc[...] / l_sc[...]
        o = o[:rows // 2] - lam * o[rows // 2:]
        o_ref[...] = _rmsnorm(o, g_ref[...]) * (1.0 - lam_init)


def _sb_sample_kernel(pt_ref, q_ref, knt_ref, vnt_ref, ck_hbm, cv_hbm, o_ref,
                      qbd_sc, c_sc, acc_sc, k_buf, v_buf, sems, *, n_tok):
    b = pl.program_id(0)
    rows, w = qbd_sc.shape
    _, group, _, page = k_buf.shape
    n_pages = pt_ref.shape[1]
    n_groups = n_pages // group
    u = _suffix_sum_matrix(page)

    def group_copies(g, slot):
        copies = []
        for i in range(group):
            pg = pt_ref[b, n_pages - 1 - (g * group + i)]
            copies.append(pltpu.make_async_copy(ck_hbm.at[pg], k_buf.at[slot, i],
                                                sems.at[slot, 0, i]))
            copies.append(pltpu.make_async_copy(cv_hbm.at[pg], v_buf.at[slot, i],
                                                sems.at[slot, 1, i]))
        return copies

    def start_group(g, slot):
        for copy in group_copies(g, slot):
            copy.start()

    def wait_group(g, slot):
        for copy in group_copies(g, slot):
            copy.wait()

    start_group(0, 0)

    def log_weights(kts, mask):
        qbd = qbd_sc[...]
        pairs = [_log2_keep(_dot(qbd, kt.astype(BF16))) for kt in kts]
        lk = jnp.concatenate([lk for lk, _ in pairs], axis=0)
        z2 = jnp.concatenate([z2 for _, z2 in pairs], axis=0)
        if mask is not None:
            lk = jnp.where(mask, lk, 0.0)
            z2 = jnp.where(mask, z2, NEG_INF)
        cs = _inclusive_suffix_sum(lk, u)
        return z2 + cs, cs[:, :1]

    def attend(e, totals, vts):
        c = c_sc[...]
        acc = acc_sc[...]
        for i, vt in enumerate(vts):
            a = jnp.exp2(e[i * rows:(i + 1) * rows] + c)
            acc = acc + _dot_nt(a.astype(BF16), vt.astype(BF16))
            c = c + totals[i * rows:(i + 1) * rows]
        c_sc[...] = c
        acc_sc[...] = acc

    row = lax.broadcasted_iota(jnp.int32, (rows, w), 0)
    lane = lax.broadcasted_iota(jnp.int32, (rows, w), 1)
    q_rows = _row_select(q_ref[...], row >> _log2(SB_HEADS))
    qbd = jnp.where((lane >> _log2(SB_HEAD_DIM)) == (row & (SB_HEADS - 1)), q_rows, 0.0)
    qbd_sc[...] = qbd.astype(BF16)
    c_sc[...] = jnp.zeros(c_sc.shape, F32)
    acc_sc[...] = jnp.zeros(acc_sc.shape, F32)
    n_new = knt_ref.shape[1]
    assert n_new == page
    tok_n = lax.broadcasted_iota(jnp.int32, (rows, n_new), 0) >> _log2(SB_HEADS)
    mask = _new_token_mask(tok_n, n_tok, b, True)
    attend(*log_weights([knt_ref[...]], mask), [vnt_ref[...]])

    def any_weight_left():
        return (jnp.max(c_sc[...]) > DEAD_LOG2_WEIGHT).astype(jnp.int32)

    def cond(state):
        g, go_on = state
        return jnp.logical_and(g < n_groups, go_on > 0)

    def body(state):
        g, _ = state
        slot = g & 1

        @pl.when(g + 1 < n_groups)
        def _():
            start_group(g + 1, 1 - slot)

        wait_group(g, slot)
        e, totals = log_weights([k_buf[slot, i] for i in range(group)], None)
        attend(e, totals, [v_buf[slot, i] for i in range(group)])
        return g + 1, any_weight_left()

    groups_done, _ = lax.while_loop(cond, body, (jnp.int32(0), any_weight_left()))

    @pl.when(groups_done < n_groups)
    def _():
        wait_group(groups_done, groups_done & 1)

    acc = acc_sc[...]
    h = lax.broadcasted_iota(jnp.int32, (SB_HEADS, w), 0)
    lane8 = lax.broadcasted_iota(jnp.int32, (SB_HEADS, w), 1)
    keep = (lane8 >> _log2(SB_HEAD_DIM)) == h
    for t in range(n_tok):
        tile = acc[t * SB_HEADS:(t + 1) * SB_HEADS]
        o_ref[t:t + 1, :] = jnp.sum(jnp.where(keep, tile, 0.0), axis=0, keepdims=True)


def _sb_sample_attention(q, new_kt, new_vt, cache_kt, cache_vt, page_table, *, group):
    n_seq, n_tok, w = q.shape
    page = cache_kt.shape[2]
    assert page_table.shape[1] % group == 0
    rows = SB_HEADS * n_tok
    seq_spec = pl.BlockSpec((None, n_tok, w), lambda b, pt: (b, 0, 0))
    hbm = pl.BlockSpec(memory_space=pl.ANY)
    grid_spec = pltpu.PrefetchScalarGridSpec(
        num_scalar_prefetch=1,
        grid=(n_seq,),
        in_specs=[seq_spec, _const_spec(new_kt.shape), _const_spec(new_vt.shape), hbm, hbm],
        out_specs=seq_spec,
        scratch_shapes=[pltpu.VMEM((rows, w), BF16),
                        pltpu.VMEM((rows, 1), F32),
                        pltpu.VMEM((rows, w), F32),
                        pltpu.VMEM((2, group, w, page), F32),
                        pltpu.VMEM((2, group, w, page), F32),
                        pltpu.SemaphoreType.DMA((2, 2, group))],
    )
    return pl.pallas_call(
        functools.partial(_sb_sample_kernel, n_tok=n_tok),
        grid_spec=grid_spec,
        out_shape=jax.ShapeDtypeStruct((n_seq, n_tok, w), F32),
        compiler_params=pltpu.CompilerParams(
            dimension_semantics=("arbitrary",), vmem_limit_bytes=VMEM_LIMIT_BYTES),
        name="sb_attn_sample",
    )(page_table, q, new_kt, new_vt, cache_kt, cache_vt)


def _diff_sample_scratch(n_tok, page, per_step):
    rows = 2 * DA_HEADS * n_tok
    return [pltpu.VMEM((rows, DA_HEADS * 2 * DA_HEAD_DIM), BF16),
            pltpu.VMEM((rows, 1), F32), pltpu.VMEM((rows, 1), F32),
            pltpu.VMEM((rows, DA_V_DIM), F32),
            pltpu.VMEM((2 * per_step * rows, page), F32)]


def _page_index(b, j, pt_ref, *, slot, per_step, n_pages, reverse, lag):
    step = jnp.clip(j - lag, 0, n_pages // per_step - 1)
    p = step * per_step + slot
    if reverse:
        p = n_pages - 1 - p
    return (pt_ref[b, p], 0, 0)


def _sample_attention(kern, q, new_k, new_v, extras, cache_k, cache_v, page_table, *,
                      reverse, out_block, scratch, name):
    n_seq, n_tok, w = q.shape
    n_pages = page_table.shape[1]
    per_step = PAGES_PER_STEP
    assert n_pages % per_step == 0
    seq_spec = pl.BlockSpec((None, n_tok, w), lambda b, j, pt: (b, 0, 0))
    out_spec = pl.BlockSpec((None, *out_block), lambda b, j, pt: (b, 0, 0))

    def page_specs(cache, lag):
        return [pl.BlockSpec((None, *cache.shape[1:]),
                             functools.partial(_page_index, slot=i, per_step=per_step,
                                               n_pages=n_pages, reverse=reverse, lag=lag))
                for i in range(per_step)]

    grid_spec = pltpu.PrefetchScalarGridSpec(
        num_scalar_prefetch=1,
        grid=(n_seq, n_pages // per_step + 1),
        in_specs=[seq_spec] + [_const_spec(e.shape) for e in (new_k, new_v, *extras)]
        + page_specs(cache_k, 0) + page_specs(cache_v, 1),
        out_specs=out_spec,
        scratch_shapes=scratch(per_step),
    )
    return pl.pallas_call(
        functools.partial(kern, n_pages_step=per_step, n_tok=n_tok),
        grid_spec=grid_spec,
        out_shape=jax.ShapeDtypeStruct((n_seq, *out_block), F32),
        compiler_params=pltpu.CompilerParams(
            dimension_semantics=("parallel", "arbitrary"),
            vmem_limit_bytes=VMEM_LIMIT_BYTES),
        name=name,
    )(page_table, q, new_k, new_v, *extras, *([cache_k] * per_step), *([cache_v] * per_step))


def _rope_tables(pos):
    half = ROPE_DIM // 2
    inv_freq = ROPE_THETA ** (-jnp.arange(0, ROPE_DIM, 2, dtype=F32) / ROPE_DIM)
    ang = pos.astype(F32)[:, None] * inv_freq[None, :]
    cos, sin = jnp.cos(ang), jnp.sin(ang)
    n = pos.shape[0]
    rest = DA_HEAD_DIM - ROPE_DIM
    zeros_h = jnp.zeros((n, half), F32)
    cos_r = jnp.concatenate([cos, cos, jnp.ones((n, rest), F32)], axis=1)
    sin_next = jnp.concatenate([-sin, zeros_h, jnp.zeros((n, rest), F32)], axis=1)
    sin_prev = jnp.concatenate([zeros_h, sin, jnp.zeros((n, rest), F32)], axis=1)
    reps = LANES // DA_HEAD_DIM
    return (tuple(jnp.tile(t, (1, reps)) for t in (cos_r, sin_next, sin_prev)),
            (cos.T, sin.T))


def kernel(x_prompt, x_sample, cache_diff_k, cache_diff_v, cache_sb_k, cache_sb_v, page_table,
           norm1, w_ffn1_gate, w_ffn1_up, w_ffn1_down, norm2, w_in,
           lambda_q1, lambda_k1, lambda_q2, lambda_k2, subln_g,
           w_branch_a, w_branch_b, w_out, norm3, w_ffn2_gate, w_ffn2_up, w_ffn2_down, norm_f):
    bsz, seq, d = x_prompt.shape
    n_seq, n_tok, _ = x_sample.shape
    depth = norm1.shape[0]
    assert depth == 1, "the final norm is fused into the last layer's second FFN"
    pool, page = cache_diff_k.shape[1:3]
    past_len = page_table.shape[1] * page
    qk_w = DA_HEADS * 2 * DA_HEAD_DIM
    v_w = DA_HEADS * DA_V_DIM
    sb_w = SB_HEADS * SB_HEAD_DIM
    n_s = n_seq * n_tok

    xp = x_prompt.reshape(bsz * seq, d)
    xs = x_sample.reshape(n_s, d)
    tabs_p = _rope_tables(jnp.arange(seq))
    tabs_s = _rope_tables(jnp.tile(past_len + jnp.arange(n_tok), n_seq))
    gf = norm_f.reshape(1, d)
    new_kv_p, new_kv_s = [], []

    for l in range(depth):
        lam_init = 0.8 - 0.6 * math.exp(-0.3 * l)
        row = lambda a: a[l][None]
        cast = lambda a: a[l].astype(BF16)
        lams = tuple(row(a) for a in (lambda_q1, lambda_k1, lambda_q2, lambda_k2))
        ffn1 = (row(norm1), cast(w_ffn1_gate), cast(w_ffn1_up), cast(w_ffn1_down))
        merge = (cast(w_branch_a), cast(w_branch_b), cast(w_out), row(norm3),
                 cast(w_ffn2_gate), cast(w_ffn2_up), cast(w_ffn2_down), gf)
        w_in_l = cast(w_in)
        w_t = jnp.concatenate(
            [w_in_l[:, o:o + qk_w].T for o in (qk_w, 2 * qk_w + v_w + sb_w,
                                               2 * qk_w + v_w + 2 * sb_w)], axis=0)
        g_sub = row(subln_g)

        xp = _ffn_half(xp, *ffn1, tm=FFN_TOKEN_TILE)
        (qa, qb, ka_t, va, kb_t, vb_t, ga, gb, ka_h, va_h, kb_h, vb_h) = _proj(
            xp, row(norm2), w_in_l, w_t, *tabs_p, tm=PROJ_TOKEN_TILE, sample=False)
        shp = lambda a: a.reshape(bsz, seq, a.shape[-1])
        oa = _diff_prompt(shp(qa), ka_h, shp(va_h), lams, g_sub, lam_init=lam_init,
                          tq=ATTN_QUERY_TILE, tk=ATTN_KEY_BLOCK)
        ob = _sb_prompt(shp(qb), kb_h, vb_h, tq=ATTN_QUERY_TILE, tk=ATTN_KEY_BLOCK)
        xp = _merge_ffn(xp, oa.reshape(bsz * seq, v_w), ob.reshape(bsz * seq, sb_w), ga, gb,
                        *merge, tm=FFN_TOKEN_TILE)
        new_kv_p.append((
            jnp.transpose(ka_t.reshape(bsz, DA_HEADS, 2, DA_HEAD_DIM, seq), (0, 4, 1, 2, 3)),
            va.reshape(bsz, seq, DA_HEADS, DA_V_DIM),
            jnp.transpose(kb_t.reshape(bsz, SB_HEADS, SB_HEAD_DIM, seq), (0, 3, 1, 2)),
            jnp.transpose(vb_t.reshape(bsz, SB_HEADS, SB_HEAD_DIM, seq), (0, 3, 1, 2))))

        xs = _ffn_half(xs, *ffn1, tm=n_s)
        (qa, qb, ka_t, va, kb_t, vb_t, ga, gb, ka, kb, vb) = _proj(
            xs, row(norm2), w_in_l, w_t, *tabs_s, tm=n_s, sample=True)
        shs = lambda a: a.reshape(n_seq, n_tok, a.shape[-1])
        ck_a = jnp.transpose(cache_diff_k[l], (0, 2, 3, 4, 1)).reshape(pool, qk_w, page)
        cv_a = cache_diff_v[l].reshape(pool, page * DA_HEADS, DA_V_DIM)
        ck_b = jnp.transpose(cache_sb_k[l], (0, 2, 3, 1)).reshape(pool, sb_w, page)
        cv_b = jnp.transpose(cache_sb_v[l], (0, 2, 3, 1)).reshape(pool, sb_w, page)
        oa = _sample_attention(
            functools.partial(_diff_sample_kernel, lam_init=lam_init),
            shs(qa), ka_t[0], va, (*lams, g_sub), ck_a, cv_a, page_table,
            reverse=False, out_block=(DA_HEADS * n_tok, DA_V_DIM),
            scratch=functools.partial(_diff_sample_scratch, n_tok, page),
            name="diff_attn_sample")
        oa = jnp.transpose(oa.reshape(n_seq, DA_HEADS, n_tok, DA_V_DIM), (0, 2, 1, 3))
        ob = _sb_sample_attention(shs(qb), kb_t[0], vb_t[0], ck_b, cv_b, page_table,
                                  group=SB_PAGE_GROUP)
        xs = _merge_ffn(xs, oa.reshape(n_s, v_w), ob.reshape(n_s, sb_w), ga, gb, *merge, tm=n_s)
        new_kv_s.append((ka.reshape(n_seq, n_tok, DA_HEADS, 2, DA_HEAD_DIM),
                         va.reshape(n_seq, n_tok, DA_HEADS, DA_V_DIM),
                         kb.reshape(n_seq, n_tok, SB_HEADS, SB_HEAD_DIM),
                         vb.reshape(n_seq, n_tok, SB_HEADS, SB_HEAD_DIM)))

    stack = lambda items, i: jnp.stack([it[i] for it in items])
    return (xp.reshape(bsz, seq, d), xs.reshape(n_seq, n_tok, d),
            *(stack(new_kv_p, i) for i in range(4)),
            *(stack(new_kv_s, i) for i in range(4)))
```

```python
import functools
import math

import jax
import jax.numpy as jnp
from jax import lax
from jax.experimental import pallas as pl
from jax.experimental.pallas import tpu as pltpu

F32 = jnp.float32
BF16 = jnp.bfloat16

DA_HEADS = 4
DA_HEAD_DIM = 64
DA_V_DIM = 2 * DA_HEAD_DIM
SB_HEADS = 8
SB_HEAD_DIM = 64
ROPE_THETA = 500000.0
ROPE_DIM = DA_HEAD_DIM // 4
FFN_RES = 0.5
RMS_EPS = 1e-6
NEG_INF = -1e30

LANES = 128
SUBLANES = 8
VMEM_LIMIT_BYTES = 56 * 1024 * 1024

FFN_TOKEN_TILE = 512
FFN_CHUNK = 1408
PROJ_TOKEN_TILE = 256
ATTN_QUERY_TILE = 512
ATTN_KEY_BLOCK = 256
PAGES_PER_STEP = 32
SB_PAGE_GROUP = 4


def _const_spec(shape):
    zeros = (0,) * len(shape)
    return pl.BlockSpec(shape, lambda *_: zeros, pipeline_mode=pl.Buffered(1))


def _log2(n):
    assert n & (n - 1) == 0
    return n.bit_length() - 1


def _rmsnorm(x, g):
    ms = jnp.mean(x * x, axis=-1, keepdims=True)
    return x * lax.rsqrt(ms + RMS_EPS) * g


def _dot(a, b):
    return jnp.dot(a, b, preferred_element_type=F32)


def _dot_nt(a, b):
    return lax.dot_general(a, b, (((1,), (1,)), ((), ())), preferred_element_type=F32)


def _swiglu(xn, wg_ref, wu_ref, wd_ref):
    xb = xn.astype(BF16)
    d_ff = wg_ref.shape[1]
    chunk = FFN_CHUNK if d_ff % FFN_CHUNK == 0 else d_ff
    acc = None
    for c in range(d_ff // chunk):
        lo, hi = c * chunk, (c + 1) * chunk
        hg = _dot(xb, wg_ref[:, lo:hi])
        hu = _dot(xb, wu_ref[:, lo:hi])
        h = (hg * jax.nn.sigmoid(hg) * hu).astype(BF16)
        part = _dot(h, wd_ref[lo:hi, :])
        acc = part if acc is None else acc + part
    return acc


def _lam_from_refs(lq1_ref, lk1_ref, lq2_ref, lk2_ref, lam_init):
    a1 = jnp.sum(lq1_ref[...] * lk1_ref[...], axis=-1, keepdims=True)
    a2 = jnp.sum(lq2_ref[...] * lk2_ref[...], axis=-1, keepdims=True)
    return jnp.exp(a1) - jnp.exp(a2) + lam_init


LOG2_E = 1.4426950408889634
DEAD_LOG2_WEIGHT = -150.0


def _log2_keep(z):
    z2 = z * LOG2_E
    nz2 = z * (-LOG2_E)
    return jnp.minimum(nz2, 0.0) - jnp.log2(1.0 + jnp.exp2(jnp.minimum(z2, nz2))), z2


def _suffix_sum_matrix(n):
    j = lax.broadcasted_iota(jnp.int32, (2 * n, n), 0) & (n - 1)
    k = lax.broadcasted_iota(jnp.int32, (2 * n, n), 1)
    return jnp.where(j >= k, 1.0, 0.0).astype(BF16)


def _inclusive_suffix_sum(x, u2):
    hi = x.astype(BF16)
    lo = (x - hi.astype(F32)).astype(BF16)
    return _dot(jnp.concatenate([hi, lo], axis=1), u2)


def _ffn_half_kernel(x_ref, g_ref, wg_ref, wu_ref, wd_ref, o_ref):
    x = x_ref[...]
    o_ref[...] = x + FFN_RES * _swiglu(_rmsnorm(x, g_ref[...]), wg_ref, wu_ref, wd_ref)


def _ffn_half(x, g, wg, wu, wd, *, tm):
    n, d = x.shape
    d_ff = wg.shape[1]
    row = pl.BlockSpec((tm, d), lambda i: (i, 0))
    return pl.pallas_call(
        _ffn_half_kernel,
        grid=(n // tm,),
        in_specs=[row, _const_spec((1, d)), _const_spec((d, d_ff)), _const_spec((d, d_ff)),
                  _const_spec((d_ff, d))],
        out_specs=row,
        out_shape=jax.ShapeDtypeStruct((n, d), F32),
        compiler_params=pltpu.CompilerParams(
            dimension_semantics=("parallel",), vmem_limit_bytes=VMEM_LIMIT_BYTES),
        name="ffn_half",
    )(x, g, wg, wu, wd)


def _rope_rows(t, cos, sin_next, sin_prev):
    half = ROPE_DIM // 2
    groups = []
    for c in range(0, t.shape[1], LANES):
        x = t[:, c:c + LANES]
        nxt = pltpu.roll(x, LANES - half, axis=1)
        prv = pltpu.roll(x, half, axis=1)
        groups.append(x * cos + nxt * sin_next + prv * sin_prev)
    return jnp.concatenate(groups, axis=1)


def _rope_cols(t, cos_t, sin_t):
    half = ROPE_DIM // 2
    assert half == SUBLANES
    pieces = []
    for r in range(0, t.shape[0], DA_HEAD_DIM):
        x1 = t[r:r + half]
        x2 = t[r + half:r + 2 * half]
        pieces += [x1 * cos_t - x2 * sin_t, x2 * cos_t + x1 * sin_t,
                   t[r + 2 * half:r + DA_HEAD_DIM]]
    return jnp.concatenate(pieces, axis=0)


_QA, _KA, _VA, _QB, _KB, _VB, _GA, _GB = range(8)


def _proj_kernel(x_ref, g_ref, w_ref, wt_ref, cos_ref, sn_ref, sp_ref, cos_t_ref, sin_t_ref,
                 *outs, offs, da_scale, sb_scale, sample):
    xb = _rmsnorm(x_ref[...], g_ref[...]).astype(BF16)

    def seg(i):
        return _dot(xb, w_ref[:, offs[i]:offs[i + 1]])

    def seg_t(i):
        w = offs[_KA + 1] - offs[_KA]
        return _dot_nt(wt_ref[i * w:(i + 1) * w, :], xb)

    qa_o, qb_o, ka_t, va_o, kb_t, vb_t, ga_o, gb_o = outs[:8]
    rope_tabs = (cos_ref[...], sn_ref[...], sp_ref[...])
    qa_o[...] = (_rope_rows(seg(_QA), *rope_tabs) * da_scale).astype(qa_o.dtype)
    qb_o[...] = (seg(_QB) * sb_scale).astype(qb_o.dtype)
    ka = _rope_cols(seg_t(0), cos_t_ref[...], sin_t_ref[...])
    kb = seg_t(1)
    vb = seg_t(2)
    va = seg(_VA)
    ka_t[...] = ka
    va_o[...] = va
    kb_t[...] = kb
    vb_t[...] = vb
    ga_o[...] = seg(_GA)
    gb_o[...] = seg(_GB)
    if sample:
        ka_o, kb_o, vb_o = outs[8:]
        ka_o[...] = _rope_rows(seg(_KA), *rope_tabs)
        kb_o[...] = seg(_KB)
        vb_o[...] = seg(_VB)
    else:
        ka_h, va_h, kb_h, vb_h = outs[8:]
        ka_h[...] = ka.astype(BF16)
        va_h[...] = va.astype(BF16)
        kb_h[...] = kb.astype(BF16)
        vb_h[...] = vb.astype(BF16)


def _proj(x, g, w_in, w_t, tabs_rows, tabs_cols, *, tm, sample):
    n, d = x.shape
    qk_w = DA_HEADS * 2 * DA_HEAD_DIM
    v_w = DA_HEADS * DA_V_DIM
    sb_w = SB_HEADS * SB_HEAD_DIM
    assert qk_w == sb_w and w_t.shape[0] == 3 * qk_w
    widths = (qk_w, qk_w, v_w, sb_w, sb_w, sb_w, d, d)
    assert sum(widths) == w_in.shape[1]
    offs = tuple(sum(widths[:i]) for i in range(len(widths) + 1))
    period = tabs_rows[0].shape[0]
    pos_blocks = period // tm
    n_batch = n // period
    q_dtype = F32 if sample else BF16

    def row(w):
        return pl.BlockSpec((tm, w), lambda i: (i, 0))

    col = pl.BlockSpec((None, qk_w, tm), lambda i: (i // pos_blocks, 0, i % pos_blocks))
    tab_r = pl.BlockSpec((tm, LANES), lambda i: (i % pos_blocks, 0))
    tab_c = pl.BlockSpec((ROPE_DIM // 2, tm), lambda i: (0, i % pos_blocks))

    def rows(w, dt):
        return jax.ShapeDtypeStruct((n, w), dt), row(w)

    def cols(dt):
        return jax.ShapeDtypeStruct((n_batch, qk_w, period), dt), col

    outs = [rows(qk_w, q_dtype), rows(sb_w, q_dtype), cols(F32), rows(v_w, F32), cols(F32),
            cols(F32), rows(d, F32), rows(d, F32)]
    if sample:
        outs += [rows(qk_w, F32), rows(sb_w, F32), rows(sb_w, F32)]
    else:
        outs += [cols(BF16), rows(v_w, BF16), cols(BF16), cols(BF16)]
    kern = functools.partial(_proj_kernel, offs=offs, da_scale=DA_HEAD_DIM ** -0.5,
                             sb_scale=SB_HEAD_DIM ** -0.5, sample=sample)
    return pl.pallas_call(
        kern,
        grid=(n // tm,),
        in_specs=[row(d), _const_spec((1, d)), _const_spec(w_in.shape), _const_spec(w_t.shape),
                  tab_r, tab_r, tab_r, tab_c, tab_c],
        out_specs=[o[1] for o in outs],
        out_shape=[o[0] for o in outs],
        compiler_params=pltpu.CompilerParams(
            dimension_semantics=("parallel",), vmem_limit_bytes=VMEM_LIMIT_BYTES),
        name="in_proj",
    )(x, g, w_in, w_t, *tabs_rows, *tabs_cols)


def _merge_ffn_kernel(x_ref, oa_ref, ob_ref, ga_ref, gb_ref, wa_ref, wb_ref, wo_ref,
                      g3_ref, wg_ref, wu_ref, wd_ref, gf_ref, o_ref):
    ya = _dot(oa_ref[...].astype(BF16), wa_ref[...])
    yb = _dot(ob_ref[...].astype(BF16), wb_ref[...])
    m = jax.nn.sigmoid(ga_ref[...]) * ya + jax.nn.sigmoid(gb_ref[...]) * yb
    x = x_ref[...] + _dot(m.astype(BF16), wo_ref[...])
    y = x + FFN_RES * _swiglu(_rmsnorm(x, g3_ref[...]), wg_ref, wu_ref, wd_ref)
    o_ref[...] = _rmsnorm(y, gf_ref[...])


def _merge_ffn(x, oa, ob, ga, gb, wa, wb, wo, g3, wg, wu, wd, gf, *, tm):
    n, d = x.shape

    def row(w):
        return pl.BlockSpec((tm, w), lambda i: (i, 0))

    consts = [wa, wb, wo, g3, wg, wu, wd, gf]
    return pl.pallas_call(
        _merge_ffn_kernel,
        grid=(n // tm,),
        in_specs=[row(d), row(oa.shape[1]), row(ob.shape[1]), row(d), row(d)]
        + [_const_spec(c.shape) for c in consts],
        out_specs=row(d),
        out_shape=jax.ShapeDtypeStruct((n, d), F32),
        compiler_params=pltpu.CompilerParams(
            dimension_semantics=("parallel",), vmem_limit_bytes=VMEM_LIMIT_BYTES),
        name="merge_ffn",
    )(x, oa, ob, ga, gb, *consts)


def _stack_halves(q):
    lane = lax.broadcasted_iota(jnp.int32, q.shape, 1)
    zero = jnp.zeros_like(q)
    return jnp.concatenate([jnp.where(lane < LANES // 2, q, zero),
                            jnp.where(lane >= LANES // 2, q, zero)], axis=0)


def _causal_keep(shape, tq, key_offset, strict):
    row = lax.broadcasted_iota(jnp.int32, shape, 0) & (tq - 1)
    key = lax.broadcasted_iota(jnp.int32, shape, 1) + key_offset
    return key < row if strict else key <= row


def _lane_chunks(x):
    return [x[:, c:c + LANES] for c in range(0, x.shape[1], LANES)]


def _sweep_key_blocks(qi, produce, consume, bufs_a, bufs_b, more_to_come=None):
    def put(bufs, vals):
        for buf, val in zip(bufs, vals):
            buf[...] = val

    def get(bufs):
        return [buf[...] for buf in bufs]

    n_full = 2 * qi
    put(bufs_b, produce(n_full + 1, 1))
    put(bufs_a, produce(n_full, 0))
    consume(n_full + 1, *get(bufs_b))

    def pair(i):
        kb = n_full - 1 - 2 * i
        put(bufs_b, produce(kb, None))
        consume(kb + 1, *get(bufs_a))
        put(bufs_a, produce(kb - 1, None))
        consume(kb, *get(bufs_b))

    if more_to_come is None:
        def body(i, carry):
            pair(i)
            return carry

        lax.fori_loop(0, qi, body, 0)
        consume(0, *get(bufs_a))
    else:
        def cond(state):
            i, go_on = state
            return jnp.logical_and(i < qi, go_on > 0)

        def body(state):
            i, _ = state
            kb = n_full - 1 - 2 * i
            put(bufs_b, produce(kb, None))
            consume(kb + 1, *get(bufs_a))

            @pl.when(more_to_come() > 0)
            def _():
                put(bufs_a, produce(kb - 1, None))
                consume(kb, *get(bufs_b))

            return i + 1, more_to_come()

        _, go_on = lax.while_loop(cond, body, (jnp.int32(0), more_to_come()))

        @pl.when(go_on > 0)
        def _():
            consume(0, *get(bufs_a))


def _diff_prompt_kernel(q_ref, kt_ref, v_ref, lq1_ref, lk1_ref, lq2_ref, lk2_ref, g_ref,
                        o_ref, mx_sc, l_sc, acc_sc, sa_sc, sb_sc, s_all, *, lam_init, tk):
    qi = pl.program_id(2)
    tq = q_ref.shape[0]
    qq = _stack_halves(q_ref[...])

    def scores(kb, causal_block):
        start = pl.multiple_of(kb * tk, tk)
        s = _dot(qq, kt_ref[:, pl.ds(start, tk)])
        if causal_block is not None:
            keep = _causal_keep(s.shape, tq, causal_block * tk, False)
            s = jnp.where(keep, s, NEG_INF)
        return (s,)

    mx_sc[...] = jnp.full(mx_sc.shape, NEG_INF, F32)

    def fold_max(kb, s):
        start = pl.multiple_of(kb * tk, tk)
        s_all[:, pl.ds(start, tk)] = s
        mx = mx_sc[...]
        for chunk in _lane_chunks(s):
            mx = jnp.maximum(mx, chunk)
        mx_sc[...] = mx

    _sweep_key_blocks(qi, scores, fold_max, (sa_sc,), (sb_sc,))
    m = jnp.max(mx_sc[...], axis=-1, keepdims=True)
    mx_sc[...] = jnp.broadcast_to(m, mx_sc.shape)

    l_sc[...] = jnp.zeros(l_sc.shape, F32)
    acc_sc[...] = jnp.zeros(acc_sc.shape, F32)

    def accumulate(kb):
        start = pl.multiple_of(kb * tk, tk)
        m_rep = mx_sc[...]
        ps = [jnp.exp(chunk - m_rep) for chunk in _lane_chunks(s_all[:, pl.ds(start, tk)])]
        l_new = l_sc[...]
        for p in ps:
            l_new = l_new + p
        l_sc[...] = l_new
        p = jnp.concatenate(ps, axis=1).astype(BF16)
        acc_sc[...] += _dot(p, v_ref[pl.ds(start, tk), :])

    def acc_body(i, carry):
        accumulate(2 * i)
        accumulate(2 * i + 1)
        return carry

    lax.fori_loop(0, qi + 1, acc_body, 0)

    lam = _lam_from_refs(lq1_ref, lk1_ref, lq2_ref, lk2_ref, lam_init)
    o = acc_sc[...] / jnp.sum(l_sc[...], axis=-1, keepdims=True)
    o = o[:tq] - lam * o[tq:]
    o_ref[...] = (_rmsnorm(o, g_ref[...]) * (1.0 - lam_init)).astype(o_ref.dtype)


def _diff_prompt(q, kt, v, lams, g, *, lam_init, tq, tk):
    b, s, w = q.shape
    heads = w // LANES
    assert tq & (tq - 1) == 0 and s % tq == 0 and tq == 2 * tk and tk % LANES == 0
    q_spec = pl.BlockSpec((None, tq, LANES), lambda bi, h, qi: (bi, qi, h))
    kt_spec = pl.BlockSpec((None, LANES, s), lambda bi, h, qi: (bi, h, 0))
    v_spec = pl.BlockSpec((None, s, LANES), lambda bi, h, qi: (bi, 0, h))
    small = [_const_spec(a.shape) for a in (*lams, g)]
    return pl.pallas_call(
        functools.partial(_diff_prompt_kernel, lam_init=lam_init, tk=tk),
        grid=(b, heads, s // tq),
        in_specs=[q_spec, kt_spec, v_spec] + small,
        out_specs=q_spec,
        out_shape=jax.ShapeDtypeStruct((b, s, w), BF16),
        scratch_shapes=[pltpu.VMEM((2 * tq, LANES), F32)] * 3
        + [pltpu.VMEM((2 * tq, tk), F32)] * 2 + [pltpu.VMEM((2 * tq, s), F32)],
        compiler_params=pltpu.CompilerParams(
            dimension_semantics=("parallel", "parallel", "arbitrary"),
            vmem_limit_bytes=VMEM_LIMIT_BYTES),
        name="diff_attn_prompt",
    )(q, kt, v, *lams, g)


def _sb_prompt_kernel(q_ref, kt_ref, vt_ref, o_ref, c_sc, acc_sc, ea_sc, ta_sc, eb_sc, tb_sc,
                      *, tk):
    qi = pl.program_id(2)
    tq = q_ref.shape[0]
    qq = _stack_halves(q_ref[...])
    u = _suffix_sum_matrix(tk)
    c_sc[...] = jnp.zeros(c_sc.shape, F32)
    acc_sc[...] = jnp.zeros(acc_sc.shape, F32)

    def log_weights(kb, causal_block):
        start = pl.multiple_of(kb * tk, tk)
        z = _dot(qq, kt_ref[:, pl.ds(start, tk)])
        lk, z2 = _log2_keep(z)
        if causal_block is not None:
            keep = _causal_keep(z.shape, tq, causal_block * tk, True)
            lk = jnp.where(keep, lk, 0.0)
            z2 = jnp.where(keep, z2, NEG_INF)
        cs = _inclusive_suffix_sum(lk, u)
        return z2 + cs, cs[:, :1]

    def accumulate(kb, e, total):
        start = pl.multiple_of(kb * tk, tk)
        a = jnp.exp2(e + c_sc[...])
        acc_sc[...] += _dot_nt(a.astype(BF16), vt_ref[:, pl.ds(start, tk)])
        c_sc[...] += total

    def any_weight_left():
        return (jnp.max(c_sc[...]) > DEAD_LOG2_WEIGHT).astype(jnp.int32)

    _sweep_key_blocks(qi, log_weights, accumulate, (ea_sc, ta_sc), (eb_sc, tb_sc),
                      more_to_come=any_weight_left)
    acc = acc_sc[...]
    lane = lax.broadcasted_iota(jnp.int32, (tq, LANES), 1)
    o_ref[...] = jnp.where(lane < LANES // 2, acc[:tq], acc[tq:]).astype(o_ref.dtype)


def _sb_prompt(q, kt, vt, *, tq, tk):
    b, s, w = q.shape
    groups = w // LANES
    assert tq & (tq - 1) == 0 and s % tq == 0 and tq == 2 * tk and tk % LANES == 0
    q_spec = pl.BlockSpec((None, tq, LANES), lambda bi, h, qi: (bi, qi, h))
    t_spec = pl.BlockSpec((None, LANES, s), lambda bi, h, qi: (bi, h, 0))
    block_bufs = [pltpu.VMEM((2 * tq, tk), F32), pltpu.VMEM((2 * tq, 1), F32)]
    return pl.pallas_call(
        functools.partial(_sb_prompt_kernel, tk=tk),
        grid=(b, groups, s // tq),
        in_specs=[q_spec, t_spec, t_spec],
        out_specs=q_spec,
        out_shape=jax.ShapeDtypeStruct((b, s, w), BF16),
        scratch_shapes=[pltpu.VMEM((2 * tq, 1), F32), pltpu.VMEM((2 * tq, LANES), F32)]
        + block_bufs + block_bufs,
        compiler_params=pltpu.CompilerParams(
            dimension_semantics=("parallel", "parallel", "arbitrary"),
            vmem_limit_bytes=VMEM_LIMIT_BYTES),
        name="sb_attn_prompt",
    )(q, kt, vt)


def _row_select(q, tok_of_row):
    out = jnp.zeros(tok_of_row.shape, q.dtype)
    for t in range(q.shape[0]):
        out = jnp.where(tok_of_row == t, jnp.broadcast_to(q[t:t + 1, :], out.shape), out)
    return out


def _new_token_mask(tok_of_row, n_tok, seq, strict):
    key = lax.broadcasted_iota(jnp.int32, tok_of_row.shape, 1)
    key_tok = key & (n_tok - 1)
    causal = key_tok < tok_of_row if strict else key_tok <= tok_of_row
    return ((key >> _log2(n_tok)) == seq) & causal


def _diff_sample_kernel(pt_ref, q_ref, knt_ref, vn_ref, lq1_ref, lk1_ref, lq2_ref, lk2_ref,
                        g_ref, *rest, n_pages_step, n_tok, lam_init):
    kt_refs = rest[:n_pages_step]
    v_refs = rest[n_pages_step:2 * n_pages_step]
    o_ref = rest[2 * n_pages_step]
    qbd_sc, m_sc, l_sc, acc_sc, s_sc = rest[2 * n_pages_step + 1:]
    b = pl.program_id(0)
    j = pl.program_id(1)
    rows, w = qbd_sc.shape
    page = kt_refs[0].shape[1]
    step_rows = n_pages_step * rows
    row_head = (lax.broadcasted_iota(jnp.int32, (rows, DA_V_DIM), 0) >> _log2(n_tok)) \
        & (DA_HEADS - 1)

    def scores(kts):
        qbd = qbd_sc[...]
        return [_dot(qbd, kt.astype(BF16)) for kt in kts]

    def attend(ss, v_heads):
        m_prev = m_sc[...]
        m_new = m_prev
        for s in ss:
            m_new = jnp.maximum(m_new, jnp.max(s, axis=-1, keepdims=True))
        alpha = jnp.exp(m_prev - m_new)
        l_new = alpha * l_sc[...]
        acc = alpha * acc_sc[...]
        for s, v_head in zip(ss, v_heads):
            p = jnp.exp(s - m_new)
            l_new = l_new + jnp.sum(p, axis=-1, keepdims=True)
            pb = p.astype(BF16)
            for h in range(DA_HEADS):
                pv = _dot(pb, v_head(h).astype(BF16))
                acc = acc + jnp.where(row_head == h, pv, 0.0)
        m_sc[...] = m_new
        l_sc[...] = l_new
        acc_sc[...] = acc

    @pl.when(j == 0)
    def _():
        row = lax.broadcasted_iota(jnp.int32, (rows, w), 0)
        lane = lax.broadcasted_iota(jnp.int32, (rows, w), 1)
        tok = row & (n_tok - 1)
        head_comp = ((row >> _log2(n_tok)) & (DA_HEADS - 1)) * 2 + (row >> _log2(n_tok * DA_HEADS))
        q_rows = _row_select(q_ref[...], tok)
        qbd = jnp.where((lane >> _log2(DA_HEAD_DIM)) == head_comp, q_rows, 0.0)
        qbd_sc[...] = qbd.astype(BF16)
        s_sc[pl.ds(step_rows, step_rows), :] = jnp.zeros((step_rows, page), F32)
        m_sc[...] = jnp.full(m_sc.shape, NEG_INF, F32)
        l_sc[...] = jnp.zeros(l_sc.shape, F32)
        acc_sc[...] = jnp.zeros(acc_sc.shape, F32)

    write_slot = pl.multiple_of((j & 1) * step_rows, step_rows)
    read_slot = pl.multiple_of(((j + 1) & 1) * step_rows, step_rows)
    prev = s_sc[pl.ds(read_slot, step_rows), :]
    s_sc[pl.ds(write_slot, step_rows), :] = jnp.concatenate(
        scores([r[...] for r in kt_refs]), axis=0)
    attend([prev[i * rows:(i + 1) * rows] for i in range(n_pages_step)],
           [lambda h, r=r: r[pl.ds(h, page, stride=DA_HEADS), :] for r in v_refs])

    @pl.when(j == 0)
    def _():
        m_sc[...] = jnp.full(m_sc.shape, NEG_INF, F32)
        l_sc[...] = jnp.zeros(l_sc.shape, F32)
        acc_sc[...] = jnp.zeros(acc_sc.shape, F32)
        n_new = knt_ref.shape[1]
        tok_n = lax.broadcasted_iota(jnp.int32, (rows, n_new), 0) & (n_tok - 1)
        mask = _new_token_mask(tok_n, n_tok, b, False)
        attend([jnp.where(mask, s, NEG_INF) for s in scores([knt_ref[...]])],
               [lambda h: vn_ref[:, h * DA_V_DIM:(h + 1) * DA_V_DIM]])

    @pl.when(j == pl.num_programs(1) - 1)
    def _():
        lam = _lam_from_refs(lq1_ref, lk1_ref, lq2_ref, lk2_ref, lam_init)
        o = acc_sc[...] / l_sc[...]
        o = o[:rows // 2] - lam * o[rows // 2:]
        o_ref[...] = _rmsnorm(o, g_ref[...]) * (1.0 - lam_init)


def _sb_sample_kernel(pt_ref, q_ref, knt_ref, vnt_ref, ck_hbm, cv_hbm, o_ref,
                      qbd_sc, c_sc, acc_sc, k_buf, v_buf, sems, *, n_tok):
    b = pl.program_id(0)
    rows, w = qbd_sc.shape
    _, group, _, page = k_buf.shape
    n_pages = pt_ref.shape[1]
    n_groups = n_pages // group
    u = _suffix_sum_matrix(page)

    def group_copies(g, slot):
        copies = []
        for i in range(group):
            pg = pt_ref[b, n_pages - 1 - (g * group + i)]
            copies.append(pltpu.make_async_copy(ck_hbm.at[pg], k_buf.at[slot, i],
                                                sems.at[slot, 0, i]))
            copies.append(pltpu.make_async_copy(cv_hbm.at[pg], v_buf.at[slot, i],
                                                sems.at[slot, 1, i]))
        return copies

    def start_group(g, slot):
        for copy in group_copies(g, slot):
            copy.start()

    def wait_group(g, slot):
        for copy in group_copies(g, slot):
            copy.wait()

    start_group(0, 0)

    def log_weights(kts, mask):
        qbd = qbd_sc[...]
        pairs = [_log2_keep(_dot(qbd, kt.astype(BF16))) for kt in kts]
        lk = jnp.concatenate([lk for lk, _ in pairs], axis=0)
        z2 = jnp.concatenate([z2 for _, z2 in pairs], axis=0)
        if mask is not None:
            lk = jnp.where(mask, lk, 0.0)
            z2 = jnp.where(mask, z2, NEG_INF)
        cs = _inclusive_suffix_sum(lk, u)
        return z2 + cs, cs[:, :1]

    def attend(e, totals, vts):
        c = c_sc[...]
        acc = acc_sc[...]
        for i, vt in enumerate(vts):
            a = jnp.exp2(e[i * rows:(i + 1) * rows] + c)
            acc = acc + _dot_nt(a.astype(BF16), vt.astype(BF16))
            c = c + totals[i * rows:(i + 1) * rows]
        c_sc[...] = c
        acc_sc[...] = acc

    row = lax.broadcasted_iota(jnp.int32, (rows, w), 0)
    lane = lax.broadcasted_iota(jnp.int32, (rows, w), 1)
    q_rows = _row_select(q_ref[...], row >> _log2(SB_HEADS))
    qbd = jnp.where((lane >> _log2(SB_HEAD_DIM)) == (row & (SB_HEADS - 1)), q_rows, 0.0)
    qbd_sc[...] = qbd.astype(BF16)
    c_sc[...] = jnp.zeros(c_sc.shape, F32)
    acc_sc[...] = jnp.zeros(acc_sc.shape, F32)
    n_new = knt_ref.shape[1]
    assert n_new == page
    tok_n = lax.broadcasted_iota(jnp.int32, (rows, n_new), 0) >> _log2(SB_HEADS)
    mask = _new_token_mask(tok_n, n_tok, b, True)
    attend(*log_weights([knt_ref[...]], mask), [vnt_ref[...]])

    def any_weight_left():
        return (jnp.max(c_sc[...]) > DEAD_LOG2_WEIGHT).astype(jnp.int32)

    def cond(state):
        g, go_on = state
        return jnp.logical_and(g < n_groups, go_on > 0)

    def body(state):
        g, _ = state
        slot = g & 1

        @pl.when(g + 1 < n_groups)
        def _():
            start_group(g + 1, 1 - slot)

        wait_group(g, slot)
        e, totals = log_weights([k_buf[slot, i] for i in range(group)], None)
        attend(e, totals, [v_buf[slot, i] for i in range(group)])
        return g + 1, any_weight_left()

    groups_done, _ = lax.while_loop(cond, body, (jnp.int32(0), any_weight_left()))

    @pl.when(groups_done < n_groups)
    def _():
        wait_group(groups_done, groups_done & 1)

    acc = acc_sc[...]
    h = lax.broadcasted_iota(jnp.int32, (SB_HEADS, w), 0)
    lane8 = lax.broadcasted_iota(jnp.int32, (SB_HEADS, w), 1)
    keep = (lane8 >> _log2(SB_HEAD_DIM)) == h
    for t in range(n_tok):
        tile = acc[t * SB_HEADS:(t + 1) * SB_HEADS]
        o_ref[t:t + 1, :] = jnp.sum(jnp.where(keep, tile, 0.0), axis=0, keepdims=True)


def _sb_sample_attention(q, new_kt, new_vt, cache_kt, cache_vt, page_table, *, group):
    n_seq, n_tok, w = q.shape
    page = cache_kt.shape[2]
    assert page_table.shape[1] % group == 0
    rows = SB_HEADS * n_tok
    seq_spec = pl.BlockSpec((None, n_tok, w), lambda b, pt: (b, 0, 0))
    hbm = pl.BlockSpec(memory_space=pl.ANY)
    grid_spec = pltpu.PrefetchScalarGridSpec(
        num_scalar_prefetch=1,
        grid=(n_seq,),
        in_specs=[seq_spec, _const_spec(new_kt.shape), _const_spec(new_vt.shape), hbm, hbm],
        out_specs=seq_spec,
        scratch_shapes=[pltpu.VMEM((rows, w), BF16),
                        pltpu.VMEM((rows, 1), F32),
                        pltpu.VMEM((rows, w), F32),
                        pltpu.VMEM((2, group, w, page), F32),
                        pltpu.VMEM((2, group, w, page), F32),
                        pltpu.SemaphoreType.DMA((2, 2, group))],
    )
    return pl.pallas_call(
        functools.partial(_sb_sample_kernel, n_tok=n_tok),
        grid_spec=grid_spec,
        out_shape=jax.ShapeDtypeStruct((n_seq, n_tok, w), F32),
        compiler_params=pltpu.CompilerParams(
            dimension_semantics=("arbitrary",), vmem_limit_bytes=VMEM_LIMIT_BYTES),
        name="sb_attn_sample",
    )(page_table, q, new_kt, new_vt, cache_kt, cache_vt)


def _diff_sample_scratch(n_tok, page, per_step):
    rows = 2 * DA_HEADS * n_tok
    return [pltpu.VMEM((rows, DA_HEADS * 2 * DA_HEAD_DIM), BF16),
            pltpu.VMEM((rows, 1), F32), pltpu.VMEM((rows, 1), F32),
            pltpu.VMEM((rows, DA_V_DIM), F32),
            pltpu.VMEM((2 * per_step * rows, page), F32)]


def _page_index(b, j, pt_ref, *, slot, per_step, n_pages, reverse, lag):
    step = jnp.clip(j - lag, 0, n_pages // per_step - 1)
    p = step * per_step + slot
    if reverse:
        p = n_pages - 1 - p
    return (pt_ref[b, p], 0, 0)


def _sample_attention(kern, q, new_k, new_v, extras, cache_k, cache_v, page_table, *,
                      reverse, out_block, scratch, name):
    n_seq, n_tok, w = q.shape
    n_pages = page_table.shape[1]
    per_step = PAGES_PER_STEP
    assert n_pages % per_step == 0
    seq_spec = pl.BlockSpec((None, n_tok, w), lambda b, j, pt: (b, 0, 0))
    out_spec = pl.BlockSpec((None, *out_block), lambda b, j, pt: (b, 0, 0))

    def page_specs(cache, lag):
        return [pl.BlockSpec((None, *cache.shape[1:]),
                             functools.partial(_page_index, slot=i, per_step=per_step,
                                               n_pages=n_pages, reverse=reverse, lag=lag))
                for i in range(per_step)]

    grid_spec = pltpu.PrefetchScalarGridSpec(
        num_scalar_prefetch=1,
        grid=(n_seq, n_pages // per_step + 1),
        in_specs=[seq_spec] + [_const_spec(e.shape) for e in (new_k, new_v, *extras)]
        + page_specs(cache_k, 0) + page_specs(cache_v, 1),
        out_specs=out_spec,
        scratch_shapes=scratch(per_step),
    )
    return pl.pallas_call(
        functools.partial(kern, n_pages_step=per_step, n_tok=n_tok),
        grid_spec=grid_spec,
        out_shape=jax.ShapeDtypeStruct((n_seq, *out_block), F32),
        compiler_params=pltpu.CompilerParams(
            dimension_semantics=("parallel", "arbitrary"),
            vmem_limit_bytes=VMEM_LIMIT_BYTES),
        name=name,
    )(page_table, q, new_k, new_v, *extras, *([cache_k] * per_step), *([cache_v] * per_step))


def _rope_tables(pos):
    half = ROPE_DIM // 2
    inv_freq = ROPE_THETA ** (-jnp.arange(0, ROPE_DIM, 2, dtype=F32) / ROPE_DIM)
    ang = pos.astype(F32)[:, None] * inv_freq[None, :]
    cos, sin = jnp.cos(ang), jnp.sin(ang)
    n = pos.shape[0]
    rest = DA_HEAD_DIM - ROPE_DIM
    zeros_h = jnp.zeros((n, half), F32)
    cos_r = jnp.concatenate([cos, cos, jnp.ones((n, rest), F32)], axis=1)
    sin_next = jnp.concatenate([-sin, zeros_h, jnp.zeros((n, rest), F32)], axis=1)
    sin_prev = jnp.concatenate([zeros_h, sin, jnp.zeros((n, rest), F32)], axis=1)
    reps = LANES // DA_HEAD_DIM
    return (tuple(jnp.tile(t, (1, reps)) for t in (cos_r, sin_next, sin_prev)),
            (cos.T, sin.T))


def kernel(x_prompt, x_sample, cache_diff_k, cache_diff_v, cache_sb_k, cache_sb_v, page_table,
           norm1, w_ffn1_gate, w_ffn1_up, w_ffn1_down, norm2, w_in,
           lambda_q1, lambda_k1, lambda_q2, lambda_k2, subln_g,
           w_branch_a, w_branch_b, w_out, norm3, w_ffn2_gate, w_ffn2_up, w_ffn2_down, norm_f):
    bsz, seq, d = x_prompt.shape
    n_seq, n_tok, _ = x_sample.shape
    depth = norm1.shape[0]
    assert depth == 1, "the final norm is fused into the last layer's second FFN"
    pool, page = cache_diff_k.shape[1:3]
    past_len = page_table.shape[1] * page
    qk_w = DA_HEADS * 2 * DA_HEAD_DIM
    v_w = DA_HEADS * DA_V_DIM
    sb_w = SB_HEADS * SB_HEAD_DIM
    n_s = n_seq * n_tok

    xp = x_prompt.reshape(bsz * seq, d)
    xs = x_sample.reshape(n_s, d)
    tabs_p = _rope_tables(jnp.arange(seq))
    tabs_s = _rope_tables(jnp.tile(past_len + jnp.arange(n_tok), n_seq))
    gf = norm_f.reshape(1, d)
    new_kv_p, new_kv_s = [], []

    for l in range(depth):
        lam_init = 0.8 - 0.6 * math.exp(-0.3 * l)
        row = lambda a: a[l][None]
        cast = lambda a: a[l].astype(BF16)
        lams = tuple(row(a) for a in (lambda_q1, lambda_k1, lambda_q2, lambda_k2))
        ffn1 = (row(norm1), cast(w_ffn1_gate), cast(w_ffn1_up), cast(w_ffn1_down))
        merge = (cast(w_branch_a), cast(w_branch_b), cast(w_out), row(norm3),
                 cast(w_ffn2_gate), cast(w_ffn2_up), cast(w_ffn2_down), gf)
        w_in_l = cast(w_in)
        w_t = jnp.concatenate(
            [w_in_l[:, o:o + qk_w].T for o in (qk_w, 2 * qk_w + v_w + sb_w,
                                               2 * qk_w + v_w + 2 * sb_w)], axis=0)
        g_sub = row(subln_g)

        xp = _ffn_half(xp, *ffn1, tm=FFN_TOKEN_TILE)
        (qa, qb, ka_t, va, kb_t, vb_t, ga, gb, ka_h, va_h, kb_h, vb_h) = _proj(
            xp, row(norm2), w_in_l, w_t, *tabs_p, tm=PROJ_TOKEN_TILE, sample=False)
        shp = lambda a: a.reshape(bsz, seq, a.shape[-1])
        oa = _diff_prompt(shp(qa), ka_h, shp(va_h), lams, g_sub, lam_init=lam_init,
                          tq=ATTN_QUERY_TILE, tk=ATTN_KEY_BLOCK)
        ob = _sb_prompt(shp(qb), kb_h, vb_h, tq=ATTN_QUERY_TILE, tk=ATTN_KEY_BLOCK)
        xp = _merge_ffn(xp, oa.reshape(bsz * seq, v_w), ob.reshape(bsz * seq, sb_w), ga, gb,
                        *merge, tm=FFN_TOKEN_TILE)
        new_kv_p.append((
            jnp.transpose(ka_t.reshape(bsz, DA_HEADS, 2, DA_HEAD_DIM, seq), (0, 4, 1, 2, 3)),
            va.reshape(bsz, seq, DA_HEADS, DA_V_DIM),
            jnp.transpose(kb_t.reshape(bsz, SB_HEADS, SB_HEAD_DIM, seq), (0, 3, 1, 2)),
            jnp.transpose(vb_t.reshape(bsz, SB_HEADS, SB_HEAD_DIM, seq), (0, 3, 1, 2))))

        xs = _ffn_half(xs, *ffn1, tm=n_s)
        (qa, qb, ka_t, va, kb_t, vb_t, ga, gb, ka, kb, vb) = _proj(
            xs, row(norm2), w_in_l, w_t, *tabs_s, tm=n_s, sample=True)
        shs = lambda a: a.reshape(n_seq, n_tok, a.shape[-1])
        ck_a = jnp.transpose(cache_diff_k[l], (0, 2, 3, 4, 1)).reshape(pool, qk_w, page)
        cv_a = cache_diff_v[l].reshape(pool, page * DA_HEADS, DA_V_DIM)
        ck_b = jnp.transpose(cache_sb_k[l], (0, 2, 3, 1)).reshape(pool, sb_w, page)
        cv_b = jnp.transpose(cache_sb_v[l], (0, 2, 3, 1)).reshape(pool, sb_w, page)
        oa = _sample_attention(
            functools.partial(_diff_sample_kernel, lam_init=lam_init),
            shs(qa), ka_t[0], va, (*lams, g_sub), ck_a, cv_a, page_table,
            reverse=False, out_block=(DA_HEADS * n_tok, DA_V_DIM),
            scratch=functools.partial(_diff_sample_scratch, n_tok, page),
            name="diff_attn_sample")
        oa = jnp.transpose(oa.reshape(n_seq, DA_HEADS, n_tok, DA_V_DIM), (0, 2, 1, 3))
        ob = _sb_sample_attention(shs(qb), kb_t[0], vb_t[0], ck_b, cv_b, page_table,
                                  group=SB_PAGE_GROUP)
        xs = _merge_ffn(xs, oa.reshape(n_s, v_w), ob.reshape(n_s, sb_w), ga, gb, *merge, tm=n_s)
        new_kv_s.append((ka.reshape(n_seq, n_tok, DA_HEADS, 2, DA_HEAD_DIM),
                         va.reshape(n_seq, n_tok, DA_HEADS, DA_V_DIM),
                         kb.reshape(n_seq, n_tok, SB_HEADS, SB_HEAD_DIM),
                         vb.reshape(n_seq, n_tok, SB_HEADS, SB_HEAD_DIM)))

    stack = lambda items, i: jnp.stack([it[i] for it in items])
    return (xp.reshape(bsz, seq, d), xs.reshape(n_seq, n_tok, d),
            *(stack(new_kv_p, i) for i in range(4)),
            *(stack(new_kv_s, i) for i in range(4)))
```

```python
import functools
import math

import jax
import jax.numpy as jnp
from jax import lax
from jax.experimental import pallas as pl
from jax.experimental.pallas import tpu as pltpu

F32 = jnp.float32
BF16 = jnp.bfloat16

DA_HEADS = 4
DA_HEAD_DIM = 64
DA_V_DIM = 2 * DA_HEAD_DIM
SB_HEADS = 8
SB_HEAD_DIM = 64
ROPE_THETA = 500000.0
ROPE_DIM = DA_HEAD_DIM // 4
FFN_RES = 0.5
RMS_EPS = 1e-6
NEG_INF = -1e30

LANES = 128
SUBLANES = 8
VMEM_LIMIT_BYTES = 56 * 1024 * 1024

FFN_TOKEN_TILE = 512
FFN_CHUNK = 1408
PROJ_TOKEN_TILE = 256
ATTN_QUERY_TILE = 512
ATTN_KEY_BLOCK = 256
PAGES_PER_STEP = 32
SB_PAGE_GROUP = 4


def _const_spec(shape):
    zeros = (0,) * len(shape)
    return pl.BlockSpec(shape, lambda *_: zeros, pipeline_mode=pl.Buffered(1))


def _log2(n):
    assert n & (n - 1) == 0
    return n.bit_length() - 1


def _rmsnorm(x, g):
    ms = jnp.mean(x * x, axis=-1, keepdims=True)
    return x * lax.rsqrt(ms + RMS_EPS) * g


def _dot(a, b):
    return jnp.dot(a, b, preferred_element_type=F32)


def _dot_nt(a, b):
    return lax.dot_general(a, b, (((1,), (1,)), ((), ())), preferred_element_type=F32)


def _swiglu(xn, wg_ref, wu_ref, wd_ref):
    xb = xn.astype(BF16)
    d_ff = wg_ref.shape[1]
    chunk = FFN_CHUNK if d_ff % FFN_CHUNK == 0 else d_ff
    acc = None
    for c in range(d_ff // chunk):
        lo, hi = c * chunk, (c + 1) * chunk
        hg = _dot(xb, wg_ref[:, lo:hi])
        hu = _dot(xb, wu_ref[:, lo:hi])
        h = (hg * jax.nn.sigmoid(hg) * hu).astype(BF16)
        part = _dot(h, wd_ref[lo:hi, :])
        acc = part if acc is None else acc + part
    return acc


def _lam_from_refs(lq1_ref, lk1_ref, lq2_ref, lk2_ref, lam_init):
    a1 = jnp.sum(lq1_ref[...] * lk1_ref[...], axis=-1, keepdims=True)
    a2 = jnp.sum(lq2_ref[...] * lk2_ref[...], axis=-1, keepdims=True)
    return jnp.exp(a1) - jnp.exp(a2) + lam_init


LOG2_E = 1.4426950408889634
DEAD_LOG2_WEIGHT = -150.0


def _log2_keep(z):
    z2 = z * LOG2_E
    nz2 = z * (-LOG2_E)
    return jnp.minimum(nz2, 0.0) - jnp.log2(1.0 + jnp.exp2(jnp.minimum(z2, nz2))), z2


def _suffix_sum_matrix(n):
    j = lax.broadcasted_iota(jnp.int32, (2 * n, n), 0) & (n - 1)
    k = lax.broadcasted_iota(jnp.int32, (2 * n, n), 1)
    return jnp.where(j >= k, 1.0, 0.0).astype(BF16)


def _inclusive_suffix_sum(x, u2):
    hi = x.astype(BF16)
    lo = (x - hi.astype(F32)).astype(BF16)
    return _dot(jnp.concatenate([hi, lo], axis=1), u2)


def _ffn_half_kernel(x_ref, g_ref, wg_ref, wu_ref, wd_ref, o_ref):
    x = x_ref[...]
    o_ref[...] = x + FFN_RES * _swiglu(_rmsnorm(x, g_ref[...]), wg_ref, wu_ref, wd_ref)


def _ffn_half(x, g, wg, wu, wd, *, tm):
    n, d = x.shape
    d_ff = wg.shape[1]
    row = pl.BlockSpec((tm, d), lambda i: (i, 0))
    return pl.pallas_call(
        _ffn_half_kernel,
        grid=(n // tm,),
        in_specs=[row, _const_spec((1, d)), _const_spec((d, d_ff)), _const_spec((d, d_ff)),
                  _const_spec((d_ff, d))],
        out_specs=row,
        out_shape=jax.ShapeDtypeStruct((n, d), F32),
        compiler_params=pltpu.CompilerParams(
            dimension_semantics=("parallel",), vmem_limit_bytes=VMEM_LIMIT_BYTES),
        name="ffn_half",
    )(x, g, wg, wu, wd)


def _rope_rows(t, cos, sin_next, sin_prev):
    half = ROPE_DIM // 2
    groups = []
    for c in range(0, t.shape[1], LANES):
        x = t[:, c:c + LANES]
        nxt = pltpu.roll(x, LANES - half, axis=1)
        prv = pltpu.roll(x, half, axis=1)
        groups.append(x * cos + nxt * sin_next + prv * sin_prev)
    return jnp.concatenate(groups, axis=1)


def _rope_cols(t, cos_t, sin_t):
    half = ROPE_DIM // 2
    assert half == SUBLANES
    pieces = []
    for r in range(0, t.shape[0], DA_HEAD_DIM):
        x1 = t[r:r + half]
        x2 = t[r + half:r + 2 * half]
        pieces += [x1 * cos_t - x2 * sin_t, x2 * cos_t + x1 * sin_t,
                   t[r + 2 * half:r + DA_HEAD_DIM]]
    return jnp.concatenate(pieces, axis=0)


_QA, _KA, _VA, _QB, _KB, _VB, _GA, _GB = range(8)


def _proj_kernel(x_ref, g_ref, w_ref, wt_ref, cos_ref, sn_ref, sp_ref, cos_t_ref, sin_t_ref,
                 *outs, offs, da_scale, sb_scale, sample):
    xb = _rmsnorm(x_ref[...], g_ref[...]).astype(BF16)

    def seg(i):
        return _dot(xb, w_ref[:, offs[i]:offs[i + 1]])

    def seg_t(i):
        w = offs[_KA + 1] - offs[_KA]
        return _dot_nt(wt_ref[i * w:(i + 1) * w, :], xb)

    qa_o, qb_o, ka_t, va_o, kb_t, vb_t, ga_o, gb_o = outs[:8]
    rope_tabs = (cos_ref[...], sn_ref[...], sp_ref[...])
    qa_o[...] = (_rope_rows(seg(_QA), *rope_tabs) * da_scale).astype(qa_o.dtype)
    qb_o[...] = (seg(_QB) * sb_scale).astype(qb_o.dtype)
    ka = _rope_cols(seg_t(0), cos_t_ref[...], sin_t_ref[...])
    kb = seg_t(1)
    vb = seg_t(2)
    va = seg(_VA)
    ka_t[...] = ka
    if sample:
        va_o[...] = va
    else:
        tm = va.shape[0]
        for h in range(DA_HEADS):
            va_o[pl.ds(h, tm, stride=DA_HEADS), :] = va[:, h * DA_V_DIM:(h + 1) * DA_V_DIM]
    kb_t[...] = kb
    vb_t[...] = vb
    ga_o[...] = seg(_GA)
    gb_o[...] = seg(_GB)
    if sample:
        ka_o, kb_o, vb_o = outs[8:]
        ka_o[...] = _rope_rows(seg(_KA), *rope_tabs)
        kb_o[...] = seg(_KB)
        vb_o[...] = seg(_VB)
    else:
        ka_h, va_h, kb_h, vb_h = outs[8:]
        ka_h[...] = ka.astype(BF16)
        va_h[...] = va.astype(BF16)
        kb_h[...] = kb.astype(BF16)
        vb_h[...] = vb.astype(BF16)


def _proj(x, g, w_in, w_t, tabs_rows, tabs_cols, *, tm, sample):
    n, d = x.shape
    qk_w = DA_HEADS * 2 * DA_HEAD_DIM
    v_w = DA_HEADS * DA_V_DIM
    sb_w = SB_HEADS * SB_HEAD_DIM
    assert qk_w == sb_w and w_t.shape[0] == 3 * qk_w
    widths = (qk_w, qk_w, v_w, sb_w, sb_w, sb_w, d, d)
    assert sum(widths) == w_in.shape[1]
    offs = tuple(sum(widths[:i]) for i in range(len(widths) + 1))
    period = tabs_rows[0].shape[0]
    pos_blocks = period // tm
    n_batch = n // period
    q_dtype = F32 if sample else BF16

    def row(w):
        return pl.BlockSpec((tm, w), lambda i: (i, 0))

    col = pl.BlockSpec((None, qk_w, tm), lambda i: (i // pos_blocks, 0, i % pos_blocks))
    tab_r = pl.BlockSpec((tm, LANES), lambda i: (i % pos_blocks, 0))
    tab_c = pl.BlockSpec((ROPE_DIM // 2, tm), lambda i: (0, i % pos_blocks))

    def rows(w, dt):
        return jax.ShapeDtypeStruct((n, w), dt), row(w)

    def cols(dt):
        return jax.ShapeDtypeStruct((n_batch, qk_w, period), dt), col

    va_out = rows(v_w, F32) if sample else (
        jax.ShapeDtypeStruct((n * DA_HEADS, DA_V_DIM), F32),
        pl.BlockSpec((tm * DA_HEADS, DA_V_DIM), lambda i: (i, 0)))
    outs = [rows(qk_w, q_dtype), rows(sb_w, q_dtype), cols(F32), va_out, cols(F32),
            cols(F32), rows(d, F32), rows(d, F32)]
    if sample:
        outs += [rows(qk_w, F32), rows(sb_w, F32), rows(sb_w, F32)]
    else:
        outs += [cols(BF16), rows(v_w, BF16), cols(BF16), cols(BF16)]
    kern = functools.partial(_proj_kernel, offs=offs, da_scale=DA_HEAD_DIM ** -0.5,
                             sb_scale=SB_HEAD_DIM ** -0.5, sample=sample)
    return pl.pallas_call(
        kern,
        grid=(n // tm,),
        in_specs=[row(d), _const_spec((1, d)), _const_spec(w_in.shape), _const_spec(w_t.shape),
                  tab_r, tab_r, tab_r, tab_c, tab_c],
        out_specs=[o[1] for o in outs],
        out_shape=[o[0] for o in outs],
        compiler_params=pltpu.CompilerParams(
            dimension_semantics=("parallel",), vmem_limit_bytes=VMEM_LIMIT_BYTES),
        name="in_proj",
    )(x, g, w_in, w_t, *tabs_rows, *tabs_cols)


def _merge_ffn_kernel(x_ref, oa_ref, ob_ref, ga_ref, gb_ref, wa_ref, wb_ref, wo_ref,
                      g3_ref, wg_ref, wu_ref, wd_ref, gf_ref, o_ref):
    ya = _dot(oa_ref[...].astype(BF16), wa_ref[...])
    yb = _dot(ob_ref[...].astype(BF16), wb_ref[...])
    m = jax.nn.sigmoid(ga_ref[...]) * ya + jax.nn.sigmoid(gb_ref[...]) * yb
    x = x_ref[...] + _dot(m.astype(BF16), wo_ref[...])
    y = x + FFN_RES * _swiglu(_rmsnorm(x, g3_ref[...]), wg_ref, wu_ref, wd_ref)
    o_ref[...] = _rmsnorm(y, gf_ref[...])


def _merge_ffn(x, oa, ob, ga, gb, wa, wb, wo, g3, wg, wu, wd, gf, *, tm):
    n, d = x.shape

    def row(w):
        return pl.BlockSpec((tm, w), lambda i: (i, 0))

    consts = [wa, wb, wo, g3, wg, wu, wd, gf]
    return pl.pallas_call(
        _merge_ffn_kernel,
        grid=(n // tm,),
        in_specs=[row(d), row(oa.shape[1]), row(ob.shape[1]), row(d), row(d)]
        + [_const_spec(c.shape) for c in consts],
        out_specs=row(d),
        out_shape=jax.ShapeDtypeStruct((n, d), F32),
        compiler_params=pltpu.CompilerParams(
            dimension_semantics=("parallel",), vmem_limit_bytes=VMEM_LIMIT_BYTES),
        name="merge_ffn",
    )(x, oa, ob, ga, gb, *consts)


def _stack_halves(q):
    lane = lax.broadcasted_iota(jnp.int32, q.shape, 1)
    zero = jnp.zeros_like(q)
    return jnp.concatenate([jnp.where(lane < LANES // 2, q, zero),
                            jnp.where(lane >= LANES // 2, q, zero)], axis=0)


def _causal_keep(shape, tq, key_offset, strict):
    row = lax.broadcasted_iota(jnp.int32, shape, 0) & (tq - 1)
    key = lax.broadcasted_iota(jnp.int32, shape, 1) + key_offset
    return key < row if strict else key <= row


def _lane_chunks(x):
    return [x[:, c:c + LANES] for c in range(0, x.shape[1], LANES)]


def _sweep_key_blocks(qi, produce, consume, bufs_a, bufs_b, more_to_come=None):
    def put(bufs, vals):
        for buf, val in zip(bufs, vals):
            buf[...] = val

    def get(bufs):
        return [buf[...] for buf in bufs]

    n_full = 2 * qi
    put(bufs_b, produce(n_full + 1, 1))
    put(bufs_a, produce(n_full, 0))
    consume(n_full + 1, *get(bufs_b))

    def pair(i):
        kb = n_full - 1 - 2 * i
        put(bufs_b, produce(kb, None))
        consume(kb + 1, *get(bufs_a))
        put(bufs_a, produce(kb - 1, None))
        consume(kb, *get(bufs_b))

    if more_to_come is None:
        def body(i, carry):
            pair(i)
            return carry

        lax.fori_loop(0, qi, body, 0)
        consume(0, *get(bufs_a))
    else:
        def cond(state):
            i, go_on = state
            return jnp.logical_and(i < qi, go_on > 0)

        def body(state):
            i, _ = state
            kb = n_full - 1 - 2 * i
            put(bufs_b, produce(kb, None))
            consume(kb + 1, *get(bufs_a))

            @pl.when(more_to_come() > 0)
            def _():
                put(bufs_a, produce(kb - 1, None))
                consume(kb, *get(bufs_b))

            return i + 1, more_to_come()

        _, go_on = lax.while_loop(cond, body, (jnp.int32(0), more_to_come()))

        @pl.when(go_on > 0)
        def _():
            consume(0, *get(bufs_a))


def _diff_prompt_kernel(q_ref, kt_ref, v_ref, lq1_ref, lk1_ref, lq2_ref, lk2_ref, g_ref,
                        o_ref, mx_sc, l_sc, acc_sc, sa_sc, sb_sc, s_all, *, lam_init, tk):
    qi = pl.program_id(2)
    tq = q_ref.shape[0]
    qq = _stack_halves(q_ref[...])

    def scores(kb, causal_block):
        start = pl.multiple_of(kb * tk, tk)
        s = _dot(qq, kt_ref[:, pl.ds(start, tk)])
        if causal_block is not None:
            keep = _causal_keep(s.shape, tq, causal_block * tk, False)
            s = jnp.where(keep, s, NEG_INF)
        return (s,)

    mx_sc[...] = jnp.full(mx_sc.shape, NEG_INF, F32)

    def fold_max(kb, s):
        start = pl.multiple_of(kb * tk, tk)
        s_all[:, pl.ds(start, tk)] = s
        mx = mx_sc[...]
        for chunk in _lane_chunks(s):
            mx = jnp.maximum(mx, chunk)
        mx_sc[...] = mx

    _sweep_key_blocks(qi, scores, fold_max, (sa_sc,), (sb_sc,))
    m = jnp.max(mx_sc[...], axis=-1, keepdims=True)
    mx_sc[...] = jnp.broadcast_to(m, mx_sc.shape)

    l_sc[...] = jnp.zeros(l_sc.shape, F32)
    acc_sc[...] = jnp.zeros(acc_sc.shape, F32)

    def accumulate(kb):
        start = pl.multiple_of(kb * tk, tk)
        m_rep = mx_sc[...]
        ps = [jnp.exp(chunk - m_rep) for chunk in _lane_chunks(s_all[:, pl.ds(start, tk)])]
        l_new = l_sc[...]
        for p in ps:
            l_new = l_new + p
        l_sc[...] = l_new
        p = jnp.concatenate(ps, axis=1).astype(BF16)
        acc_sc[...] += _dot(p, v_ref[pl.ds(start, tk), :])

    def acc_body(i, carry):
        accumulate(2 * i)
        accumulate(2 * i + 1)
        return carry

    lax.fori_loop(0, qi + 1, acc_body, 0)

    lam = _lam_from_refs(lq1_ref, lk1_ref, lq2_ref, lk2_ref, lam_init)
    o = acc_sc[...] / jnp.sum(l_sc[...], axis=-1, keepdims=True)
    o = o[:tq] - lam * o[tq:]
    o_ref[...] = (_rmsnorm(o, g_ref[...]) * (1.0 - lam_init)).astype(o_ref.dtype)


def _diff_prompt(q, kt, v, lams, g, *, lam_init, tq, tk):
    b, s, w = q.shape
    heads = w // LANES
    assert tq & (tq - 1) == 0 and s % tq == 0 and tq == 2 * tk and tk % LANES == 0
    q_spec = pl.BlockSpec((None, tq, LANES), lambda bi, h, qi: (bi, qi, h))
    kt_spec = pl.BlockSpec((None, LANES, s), lambda bi, h, qi: (bi, h, 0))
    v_spec = pl.BlockSpec((None, s, LANES), lambda bi, h, qi: (bi, 0, h))
    small = [_const_spec(a.shape) for a in (*lams, g)]
    return pl.pallas_call(
        functools.partial(_diff_prompt_kernel, lam_init=lam_init, tk=tk),
        grid=(b, heads, s // tq),
        in_specs=[q_spec, kt_spec, v_spec] + small,
        out_specs=q_spec,
        out_shape=jax.ShapeDtypeStruct((b, s, w), BF16),
        scratch_shapes=[pltpu.VMEM((2 * tq, LANES), F32)] * 3
        + [pltpu.VMEM((2 * tq, tk), F32)] * 2 + [pltpu.VMEM((2 * tq, s), F32)],
        compiler_params=pltpu.CompilerParams(
            dimension_semantics=("parallel", "parallel", "arbitrary"),
            vmem_limit_bytes=VMEM_LIMIT_BYTES),
        name="diff_attn_prompt",
    )(q, kt, v, *lams, g)


def _sb_prompt_kernel(q_ref, kt_ref, vt_ref, o_ref, c_sc, acc_sc, ea_sc, ta_sc, eb_sc, tb_sc,
                      *, tk):
    qi = pl.program_id(2)
    tq = q_ref.shape[0]
    qq = _stack_halves(q_ref[...])
    u = _suffix_sum_matrix(tk)
    c_sc[...] = jnp.zeros(c_sc.shape, F32)
    acc_sc[...] = jnp.zeros(acc_sc.shape, F32)

    def log_weights(kb, causal_block):
        start = pl.multiple_of(kb * tk, tk)
        z = _dot(qq, kt_ref[:, pl.ds(start, tk)])
        lk, z2 = _log2_keep(z)
        if causal_block is not None:
            keep = _causal_keep(z.shape, tq, causal_block * tk, True)
            lk = jnp.where(keep, lk, 0.0)
            z2 = jnp.where(keep, z2, NEG_INF)
        cs = _inclusive_suffix_sum(lk, u)
        return z2 + cs, cs[:, :1]

    def accumulate(kb, e, total):
        start = pl.multiple_of(kb * tk, tk)
        a = jnp.exp2(e + c_sc[...])
        acc_sc[...] += _dot_nt(a.astype(BF16), vt_ref[:, pl.ds(start, tk)])
        c_sc[...] += total

    def any_weight_left():
        return (jnp.max(c_sc[...]) > DEAD_LOG2_WEIGHT).astype(jnp.int32)

    _sweep_key_blocks(qi, log_weights, accumulate, (ea_sc, ta_sc), (eb_sc, tb_sc),
                      more_to_come=any_weight_left)
    acc = acc_sc[...]
    lane = lax.broadcasted_iota(jnp.int32, (tq, LANES), 1)
    o_ref[...] = jnp.where(lane < LANES // 2, acc[:tq], acc[tq:]).astype(o_ref.dtype)


def _sb_prompt(q, kt, vt, *, tq, tk):
    b, s, w = q.shape
    groups = w // LANES
    assert tq & (tq - 1) == 0 and s % tq == 0 and tq == 2 * tk and tk % LANES == 0
    q_spec = pl.BlockSpec((None, tq, LANES), lambda bi, h, qi: (bi, qi, h))
    t_spec = pl.BlockSpec((None, LANES, s), lambda bi, h, qi: (bi, h, 0))
    block_bufs = [pltpu.VMEM((2 * tq, tk), F32), pltpu.VMEM((2 * tq, 1), F32)]
    return pl.pallas_call(
        functools.partial(_sb_prompt_kernel, tk=tk),
        grid=(b, groups, s // tq),
        in_specs=[q_spec, t_spec, t_spec],
        out_specs=q_spec,
        out_shape=jax.ShapeDtypeStruct((b, s, w), BF16),
        scratch_shapes=[pltpu.VMEM((2 * tq, 1), F32), pltpu.VMEM((2 * tq, LANES), F32)]
        + block_bufs + block_bufs,
        compiler_params=pltpu.CompilerParams(
            dimension_semantics=("parallel", "parallel", "arbitrary"),
            vmem_limit_bytes=VMEM_LIMIT_BYTES),
        name="sb_attn_prompt",
    )(q, kt, vt)


def _row_select(q, tok_of_row):
    out = jnp.zeros(tok_of_row.shape, q.dtype)
    for t in range(q.shape[0]):
        out = jnp.where(tok_of_row == t, jnp.broadcast_to(q[t:t + 1, :], out.shape), out)
    return out


def _new_token_mask(tok_of_row, n_tok, seq, strict):
    key = lax.broadcasted_iota(jnp.int32, tok_of_row.shape, 1)
    key_tok = key & (n_tok - 1)
    causal = key_tok < tok_of_row if strict else key_tok <= tok_of_row
    return ((key >> _log2(n_tok)) == seq) & causal


def _diff_sample_kernel(pt_ref, q_ref, knt_ref, vn_ref, lq1_ref, lk1_ref, lq2_ref, lk2_ref,
                        g_ref, *rest, n_pages_step, n_tok, lam_init):
    kt_refs = rest[:n_pages_step]
    v_refs = rest[n_pages_step:2 * n_pages_step]
    o_ref = rest[2 * n_pages_step]
    qbd_sc, m_sc, l_sc, acc_sc = rest[2 * n_pages_step + 1:]
    b = pl.program_id(0)
    j = pl.program_id(1)
    rows, w = qbd_sc.shape
    page = kt_refs[0].shape[1]
    row_head = (lax.broadcasted_iota(jnp.int32, (rows, DA_V_DIM), 0) >> _log2(n_tok)) \
        & (DA_HEADS - 1)

    def scores(kts):
        qbd = qbd_sc[...]
        return [_dot(qbd, kt.astype(BF16)) for kt in kts]

    def attend(ss, v_heads):
        m_prev = m_sc[...]
        m_new = m_prev
        for s in ss:
            m_new = jnp.maximum(m_new, jnp.max(s, axis=-1, keepdims=True))
        alpha = jnp.exp(m_prev - m_new)
        l_new = alpha * l_sc[...]
        acc = alpha * acc_sc[...]
        for s, v_head in zip(ss, v_heads):
            p = jnp.exp(s - m_new)
            l_new = l_new + jnp.sum(p, axis=-1, keepdims=True)
            pb = p.astype(BF16)
            for h in range(DA_HEADS):
                pv = _dot(pb, v_head(h).astype(BF16))
                acc = acc + jnp.where(row_head == h, pv, 0.0)
        m_sc[...] = m_new
        l_sc[...] = l_new
        acc_sc[...] = acc

    @pl.when(j == 0)
    def _():
        row = lax.broadcasted_iota(jnp.int32, (rows, w), 0)
        lane = lax.broadcasted_iota(jnp.int32, (rows, w), 1)
        tok = row & (n_tok - 1)
        head_comp = ((row >> _log2(n_tok)) & (DA_HEADS - 1)) * 2 + (row >> _log2(n_tok * DA_HEADS))
        q_rows = _row_select(q_ref[...], tok)
        qbd = jnp.where((lane >> _log2(DA_HEAD_DIM)) == head_comp, q_rows, 0.0)
        qbd_sc[...] = qbd.astype(BF16)
        m_sc[...] = jnp.full(m_sc.shape, NEG_INF, F32)
        l_sc[...] = jnp.zeros(l_sc.shape, F32)
        acc_sc[...] = jnp.zeros(acc_sc.shape, F32)
        n_new = knt_ref.shape[1]
        tok_n = lax.broadcasted_iota(jnp.int32, (rows, n_new), 0) & (n_tok - 1)
        mask = _new_token_mask(tok_n, n_tok, b, False)
        attend([jnp.where(mask, s, NEG_INF) for s in scores([knt_ref[...]])],
               [lambda h: vn_ref[:, h * DA_V_DIM:(h + 1) * DA_V_DIM]])

    attend(scores([r[...] for r in kt_refs]),
           [lambda h, r=r: r[pl.ds(h, page, stride=DA_HEADS), :] for r in v_refs])

    @pl.when(j == pl.num_programs(1) - 1)
    def _():
        lam = _lam_from_refs(lq1_ref, lk1_ref, lq2_ref, lk2_ref, lam_init)
        o = acc_sc[...] / l_sc[...]
        o = o[:rows // 2] - lam * o[rows // 2:]
        o_ref[...] = _rmsnorm(o, g_ref[...]) * (1.0 - lam_init)


def _sb_sample_kernel(pt_ref, q_ref, knt_ref, vnt_ref, ck_hbm, cv_hbm, o_ref,
                      qbd_sc, c_sc, acc_sc, k_buf, v_buf, sems, *, n_tok):
    b = pl.program_id(0)
    rows, w = qbd_sc.shape
    _, group, _, page = k_buf.shape
    n_pages = pt_ref.shape[1]
    n_groups = n_pages // group
    u = _suffix_sum_matrix(page)

    def group_copies(g, slot):
        copies = []
        for i in range(group):
            pg = pt_ref[b, n_pages - 1 - (g * group + i)]
            copies.append(pltpu.make_async_copy(ck_hbm.at[pg], k_buf.at[slot, i],
                                                sems.at[slot, 0, i]))
            copies.append(pltpu.make_async_copy(cv_hbm.at[pg], v_buf.at[slot, i],
                                                sems.at[slot, 1, i]))
        return copies

    def start_group(g, slot):
        for copy in group_copies(g, slot):
            copy.start()

    def wait_group(g, slot):
        for copy in group_copies(g, slot):
            copy.wait()

    start_group(0, 0)

    def log_weights(kts, mask):
        qbd = qbd_sc[...]
        pairs = [_log2_keep(_dot(qbd, kt.astype(BF16))) for kt in kts]
        lk = jnp.concatenate([lk for lk, _ in pairs], axis=0)
        z2 = jnp.concatenate([z2 for _, z2 in pairs], axis=0)
        if mask is not None:
            lk = jnp.where(mask, lk, 0.0)
            z2 = jnp.where(mask, z2, NEG_INF)
        cs = _inclusive_suffix_sum(lk, u)
        return z2 + cs, cs[:, :1]

    def attend(e, totals, vts):
        c = c_sc[...]
        acc = acc_sc[...]
        for i, vt in enumerate(vts):
            a = jnp.exp2(e[i * rows:(i + 1) * rows] + c)
            acc = acc + _dot_nt(a.astype(BF16), vt.astype(BF16))
            c = c + totals[i * rows:(i + 1) * rows]
        c_sc[...] = c
        acc_sc[...] = acc

    row = lax.broadcasted_iota(jnp.int32, (rows, w), 0)
    lane = lax.broadcasted_iota(jnp.int32, (rows, w), 1)
    q_rows = _row_select(q_ref[...], row >> _log2(SB_HEADS))
    qbd = jnp.where((lane >> _log2(SB_HEAD_DIM)) == (row & (SB_HEADS - 1)), q_rows, 0.0)
    qbd_sc[...] = qbd.astype(BF16)
    c_sc[...] = jnp.zeros(c_sc.shape, F32)
    acc_sc[...] = jnp.zeros(acc_sc.shape, F32)
    n_new = knt_ref.shape[1]
    assert n_new == page
    tok_n = lax.broadcasted_iota(jnp.int32, (rows, n_new), 0) >> _log2(SB_HEADS)
    mask = _new_token_mask(tok_n, n_tok, b, True)
    attend(*log_weights([knt_ref[...]], mask), [vnt_ref[...]])

    def any_weight_left():
        return (jnp.max(c_sc[...]) > DEAD_LOG2_WEIGHT).astype(jnp.int32)

    def cond(state):
        g, go_on = state
        return jnp.logical_and(g < n_groups, go_on > 0)

    def body(state):
        g, _ = state
        slot = g & 1

        @pl.when(g + 1 < n_groups)
        def _():
            start_group(g + 1, 1 - slot)

        wait_group(g, slot)
        e, totals = log_weights([k_buf[slot, i] for i in range(group)], None)
        attend(e, totals, [v_buf[slot, i] for i in range(group)])
        return g + 1, any_weight_left()

    groups_done, _ = lax.while_loop(cond, body, (jnp.int32(0), any_weight_left()))

    @pl.when(groups_done < n_groups)
    def _():
        wait_group(groups_done, groups_done & 1)

    acc = acc_sc[...]
    h = lax.broadcasted_iota(jnp.int32, (SB_HEADS, w), 0)
    lane8 = lax.broadcasted_iota(jnp.int32, (SB_HEADS, w), 1)
    keep = (lane8 >> _log2(SB_HEAD_DIM)) == h
    for t in range(n_tok):
        tile = acc[t * SB_HEADS:(t + 1) * SB_HEADS]
        o_ref[t:t + 1, :] = jnp.sum(jnp.where(keep, tile, 0.0), axis=0, keepdims=True)


def _sb_sample_attention(q, new_kt, new_vt, cache_kt, cache_vt, page_table, *, group):
    n_seq, n_tok, w = q.shape
    page = cache_kt.shape[2]
    assert page_table.shape[1] % group == 0
    rows = SB_HEADS * n_tok
    seq_spec = pl.BlockSpec((None, n_tok, w), lambda b, pt: (b, 0, 0))
    hbm = pl.BlockSpec(memory_space=pl.ANY)
    grid_spec = pltpu.PrefetchScalarGridSpec(
        num_scalar_prefetch=1,
        grid=(n_seq,),
        in_specs=[seq_spec, _const_spec(new_kt.shape), _const_spec(new_vt.shape), hbm, hbm],
        out_specs=seq_spec,
        scratch_shapes=[pltpu.VMEM((rows, w), BF16),
                        pltpu.VMEM((rows, 1), F32),
                        pltpu.VMEM((rows, w), F32),
                        pltpu.VMEM((2, group, w, page), F32),
                        pltpu.VMEM((2, group, w, page), F32),
                        pltpu.SemaphoreType.DMA((2, 2, group))],
    )
    return pl.pallas_call(
        functools.partial(_sb_sample_kernel, n_tok=n_tok),
        grid_spec=grid_spec,
        out_shape=jax.ShapeDtypeStruct((n_seq, n_tok, w), F32),
        compiler_params=pltpu.CompilerParams(
            dimension_semantics=("arbitrary",), vmem_limit_bytes=VMEM_LIMIT_BYTES),
        name="sb_attn_sample",
    )(page_table, q, new_kt, new_vt, cache_kt, cache_vt)


def _diff_sample_attention(q, new_kt, new_v, lams, g, cache_kt, cache_v, page_table, *,
                           lam_init, per_step):
    n_seq, n_tok, w = q.shape
    n_pages = page_table.shape[1]
    assert n_pages % per_step == 0
    rows = 2 * DA_HEADS * n_tok
    out_block = (DA_HEADS * n_tok, DA_V_DIM)
    seq_spec = pl.BlockSpec((None, n_tok, w), lambda b, j, pt: (b, 0, 0))
    out_spec = pl.BlockSpec((None, *out_block), lambda b, j, pt: (b, 0, 0))

    def page_specs(cache):
        return [pl.BlockSpec((None, *cache.shape[1:]),
                             lambda b, j, pt, i=i: (pt[b, j * per_step + i], 0, 0))
                for i in range(per_step)]

    small = (new_kt, new_v, *lams, g)
    grid_spec = pltpu.PrefetchScalarGridSpec(
        num_scalar_prefetch=1,
        grid=(n_seq, n_pages // per_step),
        in_specs=[seq_spec] + [_const_spec(e.shape) for e in small]
        + page_specs(cache_kt) + page_specs(cache_v),
        out_specs=out_spec,
        scratch_shapes=[pltpu.VMEM((rows, w), BF16),
                        pltpu.VMEM((rows, 1), F32), pltpu.VMEM((rows, 1), F32),
                        pltpu.VMEM((rows, DA_V_DIM), F32)],
    )
    return pl.pallas_call(
        functools.partial(_diff_sample_kernel, n_pages_step=per_step, n_tok=n_tok,
                          lam_init=lam_init),
        grid_spec=grid_spec,
        out_shape=jax.ShapeDtypeStruct((n_seq, *out_block), F32),
        compiler_params=pltpu.CompilerParams(
            dimension_semantics=("arbitrary", "arbitrary"),
            vmem_limit_bytes=VMEM_LIMIT_BYTES),
        name="diff_attn_sample",
    )(page_table, q, *small, *([cache_kt] * per_step), *([cache_v] * per_step))


def _rope_tables(pos):
    half = ROPE_DIM // 2
    inv_freq = ROPE_THETA ** (-jnp.arange(0, ROPE_DIM, 2, dtype=F32) / ROPE_DIM)
    ang = pos.astype(F32)[:, None] * inv_freq[None, :]
    cos, sin = jnp.cos(ang), jnp.sin(ang)
    n = pos.shape[0]
    rest = DA_HEAD_DIM - ROPE_DIM
    zeros_h = jnp.zeros((n, half), F32)
    cos_r = jnp.concatenate([cos, cos, jnp.ones((n, rest), F32)], axis=1)
    sin_next = jnp.concatenate([-sin, zeros_h, jnp.zeros((n, rest), F32)], axis=1)
    sin_prev = jnp.concatenate([zeros_h, sin, jnp.zeros((n, rest), F32)], axis=1)
    reps = LANES // DA_HEAD_DIM
    return (tuple(jnp.tile(t, (1, reps)) for t in (cos_r, sin_next, sin_prev)),
            (cos.T, sin.T))


def kernel(x_prompt, x_sample, cache_diff_k, cache_diff_v, cache_sb_k, cache_sb_v, page_table,
           norm1, w_ffn1_gate, w_ffn1_up, w_ffn1_down, norm2, w_in,
           lambda_q1, lambda_k1, lambda_q2, lambda_k2, subln_g,
           w_branch_a, w_branch_b, w_out, norm3, w_ffn2_gate, w_ffn2_up, w_ffn2_down, norm_f):
    bsz, seq, d = x_prompt.shape
    n_seq, n_tok, _ = x_sample.shape
    depth = norm1.shape[0]
    assert depth == 1, "the final norm is fused into the last layer's second FFN"
    pool, page = cache_diff_k.shape[1:3]
    past_len = page_table.shape[1] * page
    qk_w = DA_HEADS * 2 * DA_HEAD_DIM
    v_w = DA_HEADS * DA_V_DIM
    sb_w = SB_HEADS * SB_HEAD_DIM
    n_s = n_seq * n_tok

    xp = x_prompt.reshape(bsz * seq, d)
    xs = x_sample.reshape(n_s, d)
    tabs_p = _rope_tables(jnp.arange(seq))
    tabs_s = _rope_tables(jnp.tile(past_len + jnp.arange(n_tok), n_seq))
    gf = norm_f.reshape(1, d)
    new_kv_p, new_kv_s = [], []

    for l in range(depth):
        lam_init = 0.8 - 0.6 * math.exp(-0.3 * l)
        row = lambda a: a[l][None]
        cast = lambda a: a[l].astype(BF16)
        lams = tuple(row(a) for a in (lambda_q1, lambda_k1, lambda_q2, lambda_k2))
        ffn1 = (row(norm1), cast(w_ffn1_gate), cast(w_ffn1_up), cast(w_ffn1_down))
        merge = (cast(w_branch_a), cast(w_branch_b), cast(w_out), row(norm3),
                 cast(w_ffn2_gate), cast(w_ffn2_up), cast(w_ffn2_down), gf)
        w_in_l = cast(w_in)
        w_t = jnp.concatenate(
            [w_in_l[:, o:o + qk_w].T for o in (qk_w, 2 * qk_w + v_w + sb_w,
                                               2 * qk_w + v_w + 2 * sb_w)], axis=0)
        g_sub = row(subln_g)

        xp = _ffn_half(xp, *ffn1, tm=FFN_TOKEN_TILE)
        (qa, qb, ka_t, va, kb_t, vb_t, ga, gb, ka_h, va_h, kb_h, vb_h) = _proj(
            xp, row(norm2), w_in_l, w_t, *tabs_p, tm=PROJ_TOKEN_TILE, sample=False)
        shp = lambda a: a.reshape(bsz, seq, a.shape[-1])
        oa = _diff_prompt(shp(qa), ka_h, shp(va_h), lams, g_sub, lam_init=lam_init,
                          tq=ATTN_QUERY_TILE, tk=ATTN_KEY_BLOCK)
        ob = _sb_prompt(shp(qb), kb_h, vb_h, tq=ATTN_QUERY_TILE, tk=ATTN_KEY_BLOCK)
        xp = _merge_ffn(xp, oa.reshape(bsz * seq, v_w), ob.reshape(bsz * seq, sb_w), ga, gb,
                        *merge, tm=FFN_TOKEN_TILE)
        new_kv_p.append((
            jnp.transpose(ka_t.reshape(bsz, DA_HEADS, 2, DA_HEAD_DIM, seq), (0, 4, 1, 2, 3)),
            va.reshape(bsz, seq, DA_HEADS, DA_V_DIM),
            jnp.transpose(kb_t.reshape(bsz, SB_HEADS, SB_HEAD_DIM, seq), (0, 3, 1, 2)),
            jnp.transpose(vb_t.reshape(bsz, SB_HEADS, SB_HEAD_DIM, seq), (0, 3, 1, 2))))

        xs = _ffn_half(xs, *ffn1, tm=n_s)
        (qa, qb, ka_t, va, kb_t, vb_t, ga, gb, ka, kb, vb) = _proj(
            xs, row(norm2), w_in_l, w_t, *tabs_s, tm=n_s, sample=True)
        shs = lambda a: a.reshape(n_seq, n_tok, a.shape[-1])
        ck_a = jnp.transpose(cache_diff_k[l], (0, 2, 3, 4, 1)).reshape(pool, qk_w, page)
        cv_a = cache_diff_v[l].reshape(pool, page * DA_HEADS, DA_V_DIM)
        ck_b = jnp.transpose(cache_sb_k[l], (0, 2, 3, 1)).reshape(pool, sb_w, page)
        cv_b = jnp.transpose(cache_sb_v[l], (0, 2, 3, 1)).reshape(pool, sb_w, page)
        oa = _diff_sample_attention(shs(qa), ka_t[0], va, lams, g_sub, ck_a, cv_a, page_table,
                                    lam_init=lam_init, per_step=PAGES_PER_STEP)
        oa = jnp.transpose(oa.reshape(n_seq, DA_HEADS, n_tok, DA_V_DIM), (0, 2, 1, 3))
        ob = _sb_sample_attention(shs(qb), kb_t[0], vb_t[0], ck_b, cv_b, page_table,
                                  group=SB_PAGE_GROUP)
        xs = _merge_ffn(xs, oa.reshape(n_s, v_w), ob.reshape(n_s, sb_w), ga, gb, *merge, tm=n_s)
        new_kv_s.append((ka.reshape(n_seq, n_tok, DA_HEADS, 2, DA_HEAD_DIM),
                         va.reshape(n_seq, n_tok, DA_HEADS, DA_V_DIM),
                         kb.reshape(n_seq, n_tok, SB_HEADS, SB_HEAD_DIM),
                         vb.reshape(n_seq, n_tok, SB_HEADS, SB_HEAD_DIM)))

    stack = lambda items, i: jnp.stack([it[i] for it in items])
    return (xp.reshape(bsz, seq, d), xs.reshape(n_seq, n_tok, d),
            *(stack(new_kv_p, i) for i in range(4)),
            *(stack(new_kv_s, i) for i in range(4)))
```

```python
import functools
import math

import jax
import jax.numpy as jnp
from jax import lax
from jax.experimental import pallas as pl
from jax.experimental.pallas import tpu as pltpu

F32 = jnp.float32
BF16 = jnp.bfloat16

DA_HEADS = 4
DA_HEAD_DIM = 64
DA_V_DIM = 2 * DA_HEAD_DIM
SB_HEADS = 8
SB_HEAD_DIM = 64
ROPE_THETA = 500000.0
ROPE_DIM = DA_HEAD_DIM // 4
FFN_RES = 0.5
RMS_EPS = 1e-6
NEG_INF = -1e30

LANES = 128
SUBLANES = 8
VMEM_LIMIT_BYTES = 56 * 1024 * 1024

FFN_TOKEN_TILE = 512
FFN_CHUNK = 1408
PROJ_TOKEN_TILE = 256
ATTN_QUERY_TILE = 512
ATTN_KEY_BLOCK = 256
PAGES_PER_STEP = 32
SB_PAGE_GROUP = 4


def _const_spec(shape):
    zeros = (0,) * len(shape)
    return pl.BlockSpec(shape, lambda *_: zeros, pipeline_mode=pl.Buffered(1))


def _log2(n):
    assert n & (n - 1) == 0
    return n.bit_length() - 1


def _rmsnorm(x, g):
    ms = jnp.mean(x * x, axis=-1, keepdims=True)
    return x * lax.rsqrt(ms + RMS_EPS) * g


def _dot(a, b):
    return jnp.dot(a, b, preferred_element_type=F32)


def _dot_nt(a, b):
    return lax.dot_general(a, b, (((1,), (1,)), ((), ())), preferred_element_type=F32)


def _swiglu(xn, wg_ref, wu_ref, wd_ref):
    xb = xn.astype(BF16)
    d_ff = wg_ref.shape[1]
    chunk = FFN_CHUNK if d_ff % FFN_CHUNK == 0 else d_ff
    acc = None
    for c in range(d_ff // chunk):
        lo, hi = c * chunk, (c + 1) * chunk
        hg = _dot(xb, wg_ref[:, lo:hi])
        hu = _dot(xb, wu_ref[:, lo:hi])
        h = (hg * jax.nn.sigmoid(hg) * hu).astype(BF16)
        part = _dot(h, wd_ref[lo:hi, :])
        acc = part if acc is None else acc + part
    return acc


def _lam_from_refs(lq1_ref, lk1_ref, lq2_ref, lk2_ref, lam_init):
    a1 = jnp.sum(lq1_ref[...] * lk1_ref[...], axis=-1, keepdims=True)
    a2 = jnp.sum(lq2_ref[...] * lk2_ref[...], axis=-1, keepdims=True)
    return jnp.exp(a1) - jnp.exp(a2) + lam_init


LOG2_E = 1.4426950408889634
DEAD_LOG2_WEIGHT = -150.0


def _log2_keep(z):
    z2 = z * LOG2_E
    nz2 = z * (-LOG2_E)
    return jnp.minimum(nz2, 0.0) - jnp.log2(1.0 + jnp.exp2(jnp.minimum(z2, nz2))), z2


def _suffix_sum_matrix(n):
    j = lax.broadcasted_iota(jnp.int32, (2 * n, n), 0) & (n - 1)
    k = lax.broadcasted_iota(jnp.int32, (2 * n, n), 1)
    return jnp.where(j >= k, 1.0, 0.0).astype(BF16)


def _inclusive_suffix_sum(x, u2):
    hi = x.astype(BF16)
    lo = (x - hi.astype(F32)).astype(BF16)
    return _dot(jnp.concatenate([hi, lo], axis=1), u2)


def _ffn_half_kernel(x_ref, g_ref, wg_ref, wu_ref, wd_ref, o_ref):
    x = x_ref[...]
    o_ref[...] = x + FFN_RES * _swiglu(_rmsnorm(x, g_ref[...]), wg_ref, wu_ref, wd_ref)


def _ffn_half(x, g, wg, wu, wd, *, tm):
    n, d = x.shape
    d_ff = wg.shape[1]
    row = pl.BlockSpec((tm, d), lambda i: (i, 0))
    return pl.pallas_call(
        _ffn_half_kernel,
        grid=(n // tm,),
        in_specs=[row, _const_spec((1, d)), _const_spec((d, d_ff)), _const_spec((d, d_ff)),
                  _const_spec((d_ff, d))],
        out_specs=row,
        out_shape=jax.ShapeDtypeStruct((n, d), F32),
        compiler_params=pltpu.CompilerParams(
            dimension_semantics=("parallel",), vmem_limit_bytes=VMEM_LIMIT_BYTES),
        name="ffn_half",
    )(x, g, wg, wu, wd)


def _rope_rows(t, cos, sin_next, sin_prev):
    half = ROPE_DIM // 2
    groups = []
    for c in range(0, t.shape[1], LANES):
        x = t[:, c:c + LANES]
        nxt = pltpu.roll(x, LANES - half, axis=1)
        prv = pltpu.roll(x, half, axis=1)
        groups.append(x * cos + nxt * sin_next + prv * sin_prev)
    return jnp.concatenate(groups, axis=1)


def _rope_cols(t, cos_t, sin_t):
    half = ROPE_DIM // 2
    assert half == SUBLANES
    pieces = []
    for r in range(0, t.shape[0], DA_HEAD_DIM):
        x1 = t[r:r + half]
        x2 = t[r + half:r + 2 * half]
        pieces += [x1 * cos_t - x2 * sin_t, x2 * cos_t + x1 * sin_t,
                   t[r + 2 * half:r + DA_HEAD_DIM]]
    return jnp.concatenate(pieces, axis=0)


_QA, _KA, _VA, _QB, _KB, _VB, _GA, _GB = range(8)


def _proj_kernel(x_ref, g_ref, w_ref, wt_ref, cos_ref, sn_ref, sp_ref, cos_t_ref, sin_t_ref,
                 *outs, offs, da_scale, sb_scale, sample):
    xb = _rmsnorm(x_ref[...], g_ref[...]).astype(BF16)

    def seg(i):
        return _dot(xb, w_ref[:, offs[i]:offs[i + 1]])

    def seg_t(i):
        w = offs[_KA + 1] - offs[_KA]
        return _dot_nt(wt_ref[i * w:(i + 1) * w, :], xb)

    qa_o, qb_o, ka_t, va_o, kb_t, vb_t, ga_o, gb_o = outs[:8]
    rope_tabs = (cos_ref[...], sn_ref[...], sp_ref[...])
    qa_o[...] = (_rope_rows(seg(_QA), *rope_tabs) * da_scale).astype(qa_o.dtype)
    qb_o[...] = (seg(_QB) * sb_scale).astype(qb_o.dtype)
    ka = _rope_cols(seg_t(0), cos_t_ref[...], sin_t_ref[...])
    kb = seg_t(1)
    vb = seg_t(2)
    va = seg(_VA)
    ka_t[...] = ka
    if sample:
        va_o[...] = va
    else:
        tm = va.shape[0]
        for h in range(DA_HEADS):
            va_o[pl.ds(h, tm, stride=DA_HEADS), :] = va[:, h * DA_V_DIM:(h + 1) * DA_V_DIM]
    kb_t[...] = kb
    vb_t[...] = vb
    ga_o[...] = seg(_GA)
    gb_o[...] = seg(_GB)
    if sample:
        ka_o, kb_o, vb_o = outs[8:]
        ka_o[...] = _rope_rows(seg(_KA), *rope_tabs)
        kb_o[...] = seg(_KB)
        vb_o[...] = seg(_VB)
    else:
        ka_h, va_h, kb_h, vb_h = outs[8:]
        ka_h[...] = ka.astype(BF16)
        va_h[...] = va.astype(BF16)
        kb_h[...] = kb.astype(BF16)
        vb_h[...] = vb.astype(BF16)


def _proj(x, g, w_in, w_t, tabs_rows, tabs_cols, *, tm, sample):
    n, d = x.shape
    qk_w = DA_HEADS * 2 * DA_HEAD_DIM
    v_w = DA_HEADS * DA_V_DIM
    sb_w = SB_HEADS * SB_HEAD_DIM
    assert qk_w == sb_w and w_t.shape[0] == 3 * qk_w
    widths = (qk_w, qk_w, v_w, sb_w, sb_w, sb_w, d, d)
    assert sum(widths) == w_in.shape[1]
    offs = tuple(sum(widths[:i]) for i in range(len(widths) + 1))
    period = tabs_rows[0].shape[0]
    pos_blocks = period // tm
    n_batch = n // period
    q_dtype = F32 if sample else BF16

    def row(w):
        return pl.BlockSpec((tm, w), lambda i: (i, 0))

    col = pl.BlockSpec((None, qk_w, tm), lambda i: (i // pos_blocks, 0, i % pos_blocks))
    tab_r = pl.BlockSpec((tm, LANES), lambda i: (i % pos_blocks, 0))
    tab_c = pl.BlockSpec((ROPE_DIM // 2, tm), lambda i: (0, i % pos_blocks))

    def rows(w, dt):
        return jax.ShapeDtypeStruct((n, w), dt), row(w)

    def cols(dt):
        return jax.ShapeDtypeStruct((n_batch, qk_w, period), dt), col

    va_out = rows(v_w, F32) if sample else (
        jax.ShapeDtypeStruct((n * DA_HEADS, DA_V_DIM), F32),
        pl.BlockSpec((tm * DA_HEADS, DA_V_DIM), lambda i: (i, 0)))
    outs = [rows(qk_w, q_dtype), rows(sb_w, q_dtype), cols(F32), va_out, cols(F32),
            cols(F32), rows(d, F32), rows(d, F32)]
    if sample:
        outs += [rows(qk_w, F32), rows(sb_w, F32), rows(sb_w, F32)]
    else:
        outs += [cols(BF16), rows(v_w, BF16), cols(BF16), cols(BF16)]
    kern = functools.partial(_proj_kernel, offs=offs, da_scale=DA_HEAD_DIM ** -0.5,
                             sb_scale=SB_HEAD_DIM ** -0.5, sample=sample)
    return pl.pallas_call(
        kern,
        grid=(n // tm,),
        in_specs=[row(d), _const_spec((1, d)), _const_spec(w_in.shape), _const_spec(w_t.shape),
                  tab_r, tab_r, tab_r, tab_c, tab_c],
        out_specs=[o[1] for o in outs],
        out_shape=[o[0] for o in outs],
        compiler_params=pltpu.CompilerParams(
            dimension_semantics=("parallel",), vmem_limit_bytes=VMEM_LIMIT_BYTES),
        name="in_proj",
    )(x, g, w_in, w_t, *tabs_rows, *tabs_cols)


def _merge_ffn_kernel(x_ref, oa_ref, ob_ref, ga_ref, gb_ref, wa_ref, wb_ref, wo_ref,
                      g3_ref, wg_ref, wu_ref, wd_ref, gf_ref, o_ref):
    ya = _dot(oa_ref[...].astype(BF16), wa_ref[...])
    yb = _dot(ob_ref[...].astype(BF16), wb_ref[...])
    m = jax.nn.sigmoid(ga_ref[...]) * ya + jax.nn.sigmoid(gb_ref[...]) * yb
    x = x_ref[...] + _dot(m.astype(BF16), wo_ref[...])
    y = x + FFN_RES * _swiglu(_rmsnorm(x, g3_ref[...]), wg_ref, wu_ref, wd_ref)
    o_ref[...] = _rmsnorm(y, gf_ref[...])


def _merge_ffn(x, oa, ob, ga, gb, wa, wb, wo, g3, wg, wu, wd, gf, *, tm):
    n, d = x.shape

    def row(w):
        return pl.BlockSpec((tm, w), lambda i: (i, 0))

    consts = [wa, wb, wo, g3, wg, wu, wd, gf]
    return pl.pallas_call(
        _merge_ffn_kernel,
        grid=(n // tm,),
        in_specs=[row(d), row(oa.shape[1]), row(ob.shape[1]), row(d), row(d)]
        + [_const_spec(c.shape) for c in consts],
        out_specs=row(d),
        out_shape=jax.ShapeDtypeStruct((n, d), F32),
        compiler_params=pltpu.CompilerParams(
            dimension_semantics=("parallel",), vmem_limit_bytes=VMEM_LIMIT_BYTES),
        name="merge_ffn",
    )(x, oa, ob, ga, gb, *consts)


def _stack_halves(q):
    lane = lax.broadcasted_iota(jnp.int32, q.shape, 1)
    zero = jnp.zeros_like(q)
    return jnp.concatenate([jnp.where(lane < LANES // 2, q, zero),
                            jnp.where(lane >= LANES // 2, q, zero)], axis=0)


def _causal_keep(shape, tq, key_offset, strict):
    row = lax.broadcasted_iota(jnp.int32, shape, 0) & (tq - 1)
    key = lax.broadcasted_iota(jnp.int32, shape, 1) + key_offset
    return key < row if strict else key <= row


def _lane_chunks(x):
    return [x[:, c:c + LANES] for c in range(0, x.shape[1], LANES)]


def _sweep_key_blocks(qi, produce, consume, bufs_a, bufs_b, more_to_come=None):
    def put(bufs, vals):
        for buf, val in zip(bufs, vals):
            buf[...] = val

    def get(bufs):
        return [buf[...] for buf in bufs]

    n_full = 2 * qi
    put(bufs_b, produce(n_full + 1, 1))
    put(bufs_a, produce(n_full, 0))
    consume(n_full + 1, *get(bufs_b))

    def pair(i):
        kb = n_full - 1 - 2 * i
        put(bufs_b, produce(kb, None))
        consume(kb + 1, *get(bufs_a))
        put(bufs_a, produce(kb - 1, None))
        consume(kb, *get(bufs_b))

    if more_to_come is None:
        def body(i, carry):
            pair(i)
            return carry

        lax.fori_loop(0, qi, body, 0)
        consume(0, *get(bufs_a))
    else:
        def cond(state):
            i, go_on = state
            return jnp.logical_and(i < qi, go_on > 0)

        def body(state):
            i, _ = state
            kb = n_full - 1 - 2 * i
            put(bufs_b, produce(kb, None))
            consume(kb + 1, *get(bufs_a))

            @pl.when(more_to_come() > 0)
            def _():
                put(bufs_a, produce(kb - 1, None))
                consume(kb, *get(bufs_b))

            return i + 1, more_to_come()

        _, go_on = lax.while_loop(cond, body, (jnp.int32(0), more_to_come()))

        @pl.when(go_on > 0)
        def _():
            consume(0, *get(bufs_a))


def _diff_prompt_kernel(q_ref, kt_ref, v_ref, lq1_ref, lk1_ref, lq2_ref, lk2_ref, g_ref,
                        o_ref, mx_sc, l_sc, acc_sc, sa_sc, sb_sc, s_all, *, lam_init, tk):
    qi = pl.program_id(2)
    tq = q_ref.shape[0]
    qq = _stack_halves(q_ref[...])

    def scores(kb, causal_block):
        start = pl.multiple_of(kb * tk, tk)
        s = _dot(qq, kt_ref[:, pl.ds(start, tk)])
        if causal_block is not None:
            keep = _causal_keep(s.shape, tq, causal_block * tk, False)
            s = jnp.where(keep, s, NEG_INF)
        return (s,)

    mx_sc[...] = jnp.full(mx_sc.shape, NEG_INF, F32)

    def fold_max(kb, s):
        start = pl.multiple_of(kb * tk, tk)
        s_all[:, pl.ds(start, tk)] = s
        mx = mx_sc[...]
        for chunk in _lane_chunks(s):
            mx = jnp.maximum(mx, chunk)
        mx_sc[...] = mx

    _sweep_key_blocks(qi, scores, fold_max, (sa_sc,), (sb_sc,))
    m = jnp.max(mx_sc[...], axis=-1, keepdims=True)
    mx_sc[...] = jnp.broadcast_to(m, mx_sc.shape)

    l_sc[...] = jnp.zeros(l_sc.shape, F32)
    acc_sc[...] = jnp.zeros(acc_sc.shape, F32)

    def accumulate(kb):
        start = pl.multiple_of(kb * tk, tk)
        m_rep = mx_sc[...]
        ps = [jnp.exp(chunk - m_rep) for chunk in _lane_chunks(s_all[:, pl.ds(start, tk)])]
        l_new = l_sc[...]
        for p in ps:
            l_new = l_new + p
        l_sc[...] = l_new
        p = jnp.concatenate(ps, axis=1).astype(BF16)
        acc_sc[...] += _dot(p, v_ref[pl.ds(start, tk), :])

    def acc_body(i, carry):
        accumulate(2 * i)
        accumulate(2 * i + 1)
        return carry

    lax.fori_loop(0, qi + 1, acc_body, 0)

    lam = _lam_from_refs(lq1_ref, lk1_ref, lq2_ref, lk2_ref, lam_init)
    o = acc_sc[...] / jnp.sum(l_sc[...], axis=-1, keepdims=True)
    o = o[:tq] - lam * o[tq:]
    o_ref[...] = (_rmsnorm(o, g_ref[...]) * (1.0 - lam_init)).astype(o_ref.dtype)


def _diff_prompt(q, kt, v, lams, g, *, lam_init, tq, tk):
    b, s, w = q.shape
    heads = w // LANES
    assert tq & (tq - 1) == 0 and s % tq == 0 and tq == 2 * tk and tk % LANES == 0
    q_spec = pl.BlockSpec((None, tq, LANES), lambda bi, h, qi: (bi, qi, h))
    kt_spec = pl.BlockSpec((None, LANES, s), lambda bi, h, qi: (bi, h, 0))
    v_spec = pl.BlockSpec((None, s, LANES), lambda bi, h, qi: (bi, 0, h))
    small = [_const_spec(a.shape) for a in (*lams, g)]
    return pl.pallas_call(
        functools.partial(_diff_prompt_kernel, lam_init=lam_init, tk=tk),
        grid=(b, heads, s // tq),
        in_specs=[q_spec, kt_spec, v_spec] + small,
        out_specs=q_spec,
        out_shape=jax.ShapeDtypeStruct((b, s, w), BF16),
        scratch_shapes=[pltpu.VMEM((2 * tq, LANES), F32)] * 3
        + [pltpu.VMEM((2 * tq, tk), F32)] * 2 + [pltpu.VMEM((2 * tq, s), F32)],
        compiler_params=pltpu.CompilerParams(
            dimension_semantics=("parallel", "parallel", "arbitrary"),
            vmem_limit_bytes=VMEM_LIMIT_BYTES),
        name="diff_attn_prompt",
    )(q, kt, v, *lams, g)


def _sb_prompt_kernel(q_ref, kt_ref, vt_ref, o_ref, c_sc, acc_sc, ea_sc, ta_sc, eb_sc, tb_sc,
                      *, tk):
    qi = pl.program_id(2)
    tq = q_ref.shape[0]
    qq = _stack_halves(q_ref[...])
    u = _suffix_sum_matrix(tk)
    c_sc[...] = jnp.zeros(c_sc.shape, F32)
    acc_sc[...] = jnp.zeros(acc_sc.shape, F32)

    def log_weights(kb, causal_block):
        start = pl.multiple_of(kb * tk, tk)
        z = _dot(qq, kt_ref[:, pl.ds(start, tk)])
        lk, z2 = _log2_keep(z)
        if causal_block is not None:
            keep = _causal_keep(z.shape, tq, causal_block * tk, True)
            lk = jnp.where(keep, lk, 0.0)
            z2 = jnp.where(keep, z2, NEG_INF)
        cs = _inclusive_suffix_sum(lk, u)
        return z2 + cs, cs[:, :1]

    def accumulate(kb, e, total):
        start = pl.multiple_of(kb * tk, tk)
        a = jnp.exp2(e + c_sc[...])
        acc_sc[...] += _dot_nt(a.astype(BF16), vt_ref[:, pl.ds(start, tk)])
        c_sc[...] += total

    def any_weight_left():
        return (jnp.max(c_sc[...]) > DEAD_LOG2_WEIGHT).astype(jnp.int32)

    _sweep_key_blocks(qi, log_weights, accumulate, (ea_sc, ta_sc), (eb_sc, tb_sc),
                      more_to_come=any_weight_left)
    acc = acc_sc[...]
    lane = lax.broadcasted_iota(jnp.int32, (tq, LANES), 1)
    o_ref[...] = jnp.where(lane < LANES // 2, acc[:tq], acc[tq:]).astype(o_ref.dtype)


def _sb_prompt(q, kt, vt, *, tq, tk):
    b, s, w = q.shape
    groups = w // LANES
    assert tq & (tq - 1) == 0 and s % tq == 0 and tq == 2 * tk and tk % LANES == 0
    q_spec = pl.BlockSpec((None, tq, LANES), lambda bi, h, qi: (bi, qi, h))
    t_spec = pl.BlockSpec((None, LANES, s), lambda bi, h, qi: (bi, h, 0))
    block_bufs = [pltpu.VMEM((2 * tq, tk), F32), pltpu.VMEM((2 * tq, 1), F32)]
    return pl.pallas_call(
        functools.partial(_sb_prompt_kernel, tk=tk),
        grid=(b, groups, s // tq),
        in_specs=[q_spec, t_spec, t_spec],
        out_specs=q_spec,
        out_shape=jax.ShapeDtypeStruct((b, s, w), BF16),
        scratch_shapes=[pltpu.VMEM((2 * tq, 1), F32), pltpu.VMEM((2 * tq, LANES), F32)]
        + block_bufs + block_bufs,
        compiler_params=pltpu.CompilerParams(
            dimension_semantics=("parallel", "parallel", "arbitrary"),
            vmem_limit_bytes=VMEM_LIMIT_BYTES),
        name="sb_attn_prompt",
    )(q, kt, vt)


def _row_select(q, tok_of_row):
    out = jnp.zeros(tok_of_row.shape, q.dtype)
    for t in range(q.shape[0]):
        out = jnp.where(tok_of_row == t, jnp.broadcast_to(q[t:t + 1, :], out.shape), out)
    return out


def _new_token_mask(tok_of_row, n_tok, seq, strict):
    key = lax.broadcasted_iota(jnp.int32, tok_of_row.shape, 1)
    key_tok = key & (n_tok - 1)
    causal = key_tok < tok_of_row if strict else key_tok <= tok_of_row
    return ((key >> _log2(n_tok)) == seq) & causal


def _diff_sample_kernel(pt_ref, q_ref, knt_ref, vn_ref, lq1_ref, lk1_ref, lq2_ref, lk2_ref,
                        g_ref, *rest, n_pages_step, n_tok, lam_init):
    kt_refs = rest[:n_pages_step]
    v_refs = rest[n_pages_step:2 * n_pages_step]
    o_ref = rest[2 * n_pages_step]
    qbd_sc, m_sc, l_sc, acc_sc = rest[2 * n_pages_step + 1:]
    b = pl.program_id(0)
    j = pl.program_id(1)
    rows, w = qbd_sc.shape
    page = kt_refs[0].shape[1]
    row_head = (lax.broadcasted_iota(jnp.int32, (rows, DA_V_DIM), 0) >> _log2(n_tok)) \
        & (DA_HEADS - 1)

    def scores(kts):
        qbd = qbd_sc[...]
        return [_dot(qbd, kt.astype(BF16)) for kt in kts]

    def attend(ss, v_heads):
        m_prev = m_sc[...]
        m_new = m_prev
        for s in ss:
            m_new = jnp.maximum(m_new, jnp.max(s, axis=-1, keepdims=True))
        alpha = jnp.exp(m_prev - m_new)
        l_new = alpha * l_sc[...]
        acc = alpha * acc_sc[...]
        for s, v_head in zip(ss, v_heads):
            p = jnp.exp(s - m_new)
            l_new = l_new + jnp.sum(p, axis=-1, keepdims=True)
            pb = p.astype(BF16)
            for h in range(DA_HEADS):
                pv = _dot(pb, v_head(h).astype(BF16))
                acc = acc + jnp.where(row_head == h, pv, 0.0)
        m_sc[...] = m_new
        l_sc[...] = l_new
        acc_sc[...] = acc

    @pl.when(j == 0)
    def _():
        row = lax.broadcasted_iota(jnp.int32, (rows, w), 0)
        lane = lax.broadcasted_iota(jnp.int32, (rows, w), 1)
        tok = row & (n_tok - 1)
        head_comp = ((row >> _log2(n_tok)) & (DA_HEADS - 1)) * 2 + (row >> _log2(n_tok * DA_HEADS))
        q_rows = _row_select(q_ref[...], tok)
        qbd = jnp.where((lane >> _log2(DA_HEAD_DIM)) == head_comp, q_rows, 0.0)
        qbd_sc[...] = qbd.astype(BF16)
        m_sc[...] = jnp.full(m_sc.shape, NEG_INF, F32)
        l_sc[...] = jnp.zeros(l_sc.shape, F32)
        acc_sc[...] = jnp.zeros(acc_sc.shape, F32)
        n_new = knt_ref.shape[1]
        tok_n = lax.broadcasted_iota(jnp.int32, (rows, n_new), 0) & (n_tok - 1)
        mask = _new_token_mask(tok_n, n_tok, b, False)
        attend([jnp.where(mask, s, NEG_INF) for s in scores([knt_ref[...]])],
               [lambda h: vn_ref[:, h * DA_V_DIM:(h + 1) * DA_V_DIM]])

    attend(scores([r[...] for r in kt_refs]),
           [lambda h, r=r: r[pl.ds(h, page, stride=DA_HEADS), :] for r in v_refs])

    @pl.when(j == pl.num_programs(1) - 1)
    def _():
        lam = _lam_from_refs(lq1_ref, lk1_ref, lq2_ref, lk2_ref, lam_init)
        o = acc_sc[...] / l_sc[...]
        o = o[:rows // 2] - lam * o[rows // 2:]
        o_ref[...] = _rmsnorm(o, g_ref[...]) * (1.0 - lam_init)


def _sb_sample_kernel(pt_ref, q_ref, knt_ref, vnt_ref, ck_hbm, cv_hbm, o_ref,
                      qbd_sc, c_sc, acc_sc, k_buf, v_buf, sems, *, n_tok):
    b = pl.program_id(0)
    rows, w = qbd_sc.shape
    _, group, _, page = k_buf.shape
    n_pages = pt_ref.shape[1]
    n_groups = n_pages // group
    u = _suffix_sum_matrix(page)

    def group_copies(g, slot, seq=b):
        copies = []
        for i in range(group):
            pg = pt_ref[seq, n_pages - 1 - (g * group + i)]
            copies.append(pltpu.make_async_copy(ck_hbm.at[pg], k_buf.at[slot, i],
                                                sems.at[slot, 0, i]))
            copies.append(pltpu.make_async_copy(cv_hbm.at[pg], v_buf.at[slot, i],
                                                sems.at[slot, 1, i]))
        return copies

    def start_group(g, slot, seq=b):
        for copy in group_copies(g, slot, seq):
            copy.start()

    def wait_group(g, slot):
        for copy in group_copies(g, slot):
            copy.wait()

    @pl.when(b == 0)
    def _():
        start_group(0, 0)

    def log_weights(kts, mask):
        qbd = qbd_sc[...]
        pairs = [_log2_keep(_dot(qbd, kt.astype(BF16))) for kt in kts]
        lk = jnp.concatenate([lk for lk, _ in pairs], axis=0)
        z2 = jnp.concatenate([z2 for _, z2 in pairs], axis=0)
        if mask is not None:
            lk = jnp.where(mask, lk, 0.0)
            z2 = jnp.where(mask, z2, NEG_INF)
        cs = _inclusive_suffix_sum(lk, u)
        return z2 + cs, cs[:, :1]

    def attend(e, totals, vts):
        c = c_sc[...]
        acc = acc_sc[...]
        for i, vt in enumerate(vts):
            a = jnp.exp2(e[i * rows:(i + 1) * rows] + c)
            acc = acc + _dot_nt(a.astype(BF16), vt.astype(BF16))
            c = c + totals[i * rows:(i + 1) * rows]
        c_sc[...] = c
        acc_sc[...] = acc

    row = lax.broadcasted_iota(jnp.int32, (rows, w), 0)
    lane = lax.broadcasted_iota(jnp.int32, (rows, w), 1)
    q_rows = _row_select(q_ref[...], row >> _log2(SB_HEADS))
    qbd = jnp.where((lane >> _log2(SB_HEAD_DIM)) == (row & (SB_HEADS - 1)), q_rows, 0.0)
    qbd_sc[...] = qbd.astype(BF16)
    c_sc[...] = jnp.zeros(c_sc.shape, F32)
    acc_sc[...] = jnp.zeros(acc_sc.shape, F32)
    n_new = knt_ref.shape[1]
    assert n_new == page
    tok_n = lax.broadcasted_iota(jnp.int32, (rows, n_new), 0) >> _log2(SB_HEADS)
    mask = _new_token_mask(tok_n, n_tok, b, True)
    attend(*log_weights([knt_ref[...]], mask), [vnt_ref[...]])

    def any_weight_left():
        return (jnp.max(c_sc[...]) > DEAD_LOG2_WEIGHT).astype(jnp.int32)

    def cond(state):
        g, go_on = state
        return jnp.logical_and(g < n_groups, go_on > 0)

    def body(state):
        g, _ = state
        slot = g & 1

        @pl.when(g + 1 < n_groups)
        def _():
            start_group(g + 1, 1 - slot)

        wait_group(g, slot)
        e, totals = log_weights([k_buf[slot, i] for i in range(group)], None)
        attend(e, totals, [v_buf[slot, i] for i in range(group)])
        return g + 1, any_weight_left()

    groups_done, _ = lax.while_loop(cond, body, (jnp.int32(0), any_weight_left()))

    @pl.when(groups_done < n_groups)
    def _():
        wait_group(groups_done, groups_done & 1)

    @pl.when(b + 1 < pl.num_programs(0))
    def _():
        start_group(0, 0, b + 1)

    acc = acc_sc[...]
    h = lax.broadcasted_iota(jnp.int32, (SB_HEADS, w), 0)
    lane8 = lax.broadcasted_iota(jnp.int32, (SB_HEADS, w), 1)
    keep = (lane8 >> _log2(SB_HEAD_DIM)) == h
    for t in range(n_tok):
        tile = acc[t * SB_HEADS:(t + 1) * SB_HEADS]
        o_ref[t:t + 1, :] = jnp.sum(jnp.where(keep, tile, 0.0), axis=0, keepdims=True)


def _sb_sample_attention(q, new_kt, new_vt, cache_kt, cache_vt, page_table, *, group):
    n_seq, n_tok, w = q.shape
    page = cache_kt.shape[2]
    assert page_table.shape[1] % group == 0
    rows = SB_HEADS * n_tok
    seq_spec = pl.BlockSpec((None, n_tok, w), lambda b, pt: (b, 0, 0))
    hbm = pl.BlockSpec(memory_space=pl.ANY)
    grid_spec = pltpu.PrefetchScalarGridSpec(
        num_scalar_prefetch=1,
        grid=(n_seq,),
        in_specs=[seq_spec, _const_spec(new_kt.shape), _const_spec(new_vt.shape), hbm, hbm],
        out_specs=seq_spec,
        scratch_shapes=[pltpu.VMEM((rows, w), BF16),
                        pltpu.VMEM((rows, 1), F32),
                        pltpu.VMEM((rows, w), F32),
                        pltpu.VMEM((2, group, w, page), F32),
                        pltpu.VMEM((2, group, w, page), F32),
                        pltpu.SemaphoreType.DMA((2, 2, group))],
    )
    return pl.pallas_call(
        functools.partial(_sb_sample_kernel, n_tok=n_tok),
        grid_spec=grid_spec,
        out_shape=jax.ShapeDtypeStruct((n_seq, n_tok, w), F32),
        compiler_params=pltpu.CompilerParams(
            dimension_semantics=("arbitrary",), vmem_limit_bytes=VMEM_LIMIT_BYTES),
        name="sb_attn_sample",
    )(page_table, q, new_kt, new_vt, cache_kt, cache_vt)


def _diff_sample_attention(q, new_kt, new_v, lams, g, cache_kt, cache_v, page_table, *,
                           lam_init, per_step):
    n_seq, n_tok, w = q.shape
    n_pages = page_table.shape[1]
    assert n_pages % per_step == 0
    rows = 2 * DA_HEADS * n_tok
    out_block = (DA_HEADS * n_tok, DA_V_DIM)
    seq_spec = pl.BlockSpec((None, n_tok, w), lambda b, j, pt: (b, 0, 0))
    out_spec = pl.BlockSpec((None, *out_block), lambda b, j, pt: (b, 0, 0))

    def page_specs(cache):
        return [pl.BlockSpec((None, *cache.shape[1:]),
                             lambda b, j, pt, i=i: (pt[b, j * per_step + i], 0, 0))
                for i in range(per_step)]

    small = (new_kt, new_v, *lams, g)
    grid_spec = pltpu.PrefetchScalarGridSpec(
        num_scalar_prefetch=1,
        grid=(n_seq, n_pages // per_step),
        in_specs=[seq_spec] + [_const_spec(e.shape) for e in small]
        + page_specs(cache_kt) + page_specs(cache_v),
        out_specs=out_spec,
        scratch_shapes=[pltpu.VMEM((rows, w), BF16),
                        pltpu.VMEM((rows, 1), F32), pltpu.VMEM((rows, 1), F32),
                        pltpu.VMEM((rows, DA_V_DIM), F32)],
    )
    return pl.pallas_call(
        functools.partial(_diff_sample_kernel, n_pages_step=per_step, n_tok=n_tok,
                          lam_init=lam_init),
        grid_spec=grid_spec,
        out_shape=jax.ShapeDtypeStruct((n_seq, *out_block), F32),
        compiler_params=pltpu.CompilerParams(
            dimension_semantics=("arbitrary", "arbitrary"),
            vmem_limit_bytes=VMEM_LIMIT_BYTES),
        name="diff_attn_sample",
    )(page_table, q, *small, *([cache_kt] * per_step), *([cache_v] * per_step))


def _rope_tables(pos):
    half = ROPE_DIM // 2
    inv_freq = ROPE_THETA ** (-jnp.arange(0, ROPE_DIM, 2, dtype=F32) / ROPE_DIM)
    ang = pos.astype(F32)[:, None] * inv_freq[None, :]
    cos, sin = jnp.cos(ang), jnp.sin(ang)
    n = pos.shape[0]
    rest = DA_HEAD_DIM - ROPE_DIM
    zeros_h = jnp.zeros((n, half), F32)
    cos_r = jnp.concatenate([cos, cos, jnp.ones((n, rest), F32)], axis=1)
    sin_next = jnp.concatenate([-sin, zeros_h, jnp.zeros((n, rest), F32)], axis=1)
    sin_prev = jnp.concatenate([zeros_h, sin, jnp.zeros((n, rest), F32)], axis=1)
    reps = LANES // DA_HEAD_DIM
    return (tuple(jnp.tile(t, (1, reps)) for t in (cos_r, sin_next, sin_prev)),
            (cos.T, sin.T))


def kernel(x_prompt, x_sample, cache_diff_k, cache_diff_v, cache_sb_k, cache_sb_v, page_table,
           norm1, w_ffn1_gate, w_ffn1_up, w_ffn1_down, norm2, w_in,
           lambda_q1, lambda_k1, lambda_q2, lambda_k2, subln_g,
           w_branch_a, w_branch_b, w_out, norm3, w_ffn2_gate, w_ffn2_up, w_ffn2_down, norm_f):
    bsz, seq, d = x_prompt.shape
    n_seq, n_tok, _ = x_sample.shape
    depth = norm1.shape[0]
    assert depth == 1, "the final norm is fused into the last layer's second FFN"
    pool, page = cache_diff_k.shape[1:3]
    past_len = page_table.shape[1] * page
    qk_w = DA_HEADS * 2 * DA_HEAD_DIM
    v_w = DA_HEADS * DA_V_DIM
    sb_w = SB_HEADS * SB_HEAD_DIM
    n_s = n_seq * n_tok

    xp = x_prompt.reshape(bsz * seq, d)
    xs = x_sample.reshape(n_s, d)
    tabs_p = _rope_tables(jnp.arange(seq))
    tabs_s = _rope_tables(jnp.tile(past_len + jnp.arange(n_tok), n_seq))
    gf = norm_f.reshape(1, d)
    new_kv_p, new_kv_s = [], []

    for l in range(depth):
        lam_init = 0.8 - 0.6 * math.exp(-0.3 * l)
        row = lambda a: a[l][None]
        cast = lambda a: a[l].astype(BF16)
        lams = tuple(row(a) for a in (lambda_q1, lambda_k1, lambda_q2, lambda_k2))
        ffn1 = (row(norm1), cast(w_ffn1_gate), cast(w_ffn1_up), cast(w_ffn1_down))
        merge = (cast(w_branch_a), cast(w_branch_b), cast(w_out), row(norm3),
                 cast(w_ffn2_gate), cast(w_ffn2_up), cast(w_ffn2_down), gf)
        w_in_l = cast(w_in)
        w_t = jnp.concatenate(
            [w_in_l[:, o:o + qk_w].T for o in (qk_w, 2 * qk_w + v_w + sb_w,
                                               2 * qk_w + v_w + 2 * sb_w)], axis=0)
        g_sub = row(subln_g)

        xp = _ffn_half(xp, *ffn1, tm=FFN_TOKEN_TILE)
        (qa, qb, ka_t, va, kb_t, vb_t, ga, gb, ka_h, va_h, kb_h, vb_h) = _proj(
            xp, row(norm2), w_in_l, w_t, *tabs_p, tm=PROJ_TOKEN_TILE, sample=False)
        shp = lambda a: a.reshape(bsz, seq, a.shape[-1])
        oa = _diff_prompt(shp(qa), ka_h, shp(va_h), lams, g_sub, lam_init=lam_init,
                          tq=ATTN_QUERY_TILE, tk=ATTN_KEY_BLOCK)
        ob = _sb_prompt(shp(qb), kb_h, vb_h, tq=ATTN_QUERY_TILE, tk=ATTN_KEY_BLOCK)
        xp = _merge_ffn(xp, oa.reshape(bsz * seq, v_w), ob.reshape(bsz * seq, sb_w), ga, gb,
                        *merge, tm=FFN_TOKEN_TILE)
        new_kv_p.append((
            jnp.transpose(ka_t.reshape(bsz, DA_HEADS, 2, DA_HEAD_DIM, seq), (0, 4, 1, 2, 3)),
            va.reshape(bsz, seq, DA_HEADS, DA_V_DIM),
            jnp.transpose(kb_t.reshape(bsz, SB_HEADS, SB_HEAD_DIM, seq), (0, 3, 1, 2)),
            jnp.transpose(vb_t.reshape(bsz, SB_HEADS, SB_HEAD_DIM, seq), (0, 3, 1, 2))))

        xs = _ffn_half(xs, *ffn1, tm=n_s)
        (qa, qb, ka_t, va, kb_t, vb_t, ga, gb, ka, kb, vb) = _proj(
            xs, row(norm2), w_in_l, w_t, *tabs_s, tm=n_s, sample=True)
        shs = lambda a: a.reshape(n_seq, n_tok, a.shape[-1])
        ck_a = jnp.transpose(cache_diff_k[l], (0, 2, 3, 4, 1)).reshape(pool, qk_w, page)
        cv_a = cache_diff_v[l].reshape(pool, page * DA_HEADS, DA_V_DIM)
        ck_b = jnp.transpose(cache_sb_k[l], (0, 2, 3, 1)).reshape(pool, sb_w, page)
        cv_b = jnp.transpose(cache_sb_v[l], (0, 2, 3, 1)).reshape(pool, sb_w, page)
        oa = _diff_sample_attention(shs(qa), ka_t[0], va, lams, g_sub, ck_a, cv_a, page_table,
                                    lam_init=lam_init, per_step=PAGES_PER_STEP)
        oa = jnp.transpose(oa.reshape(n_seq, DA_HEADS, n_tok, DA_V_DIM), (0, 2, 1, 3))
        ob = _sb_sample_attention(shs(qb), kb_t[0], vb_t[0], ck_b, cv_b, page_table,
                                  group=SB_PAGE_GROUP)
        xs = _merge_ffn(xs, oa.reshape(n_s, v_w), ob.reshape(n_s, sb_w), ga, gb, *merge, tm=n_s)
        new_kv_s.append((ka.reshape(n_seq, n_tok, DA_HEADS, 2, DA_HEAD_DIM),
                         va.reshape(n_seq, n_tok, DA_HEADS, DA_V_DIM),
                         kb.reshape(n_seq, n_tok, SB_HEADS, SB_HEAD_DIM),
                         vb.reshape(n_seq, n_tok, SB_HEADS, SB_HEAD_DIM)))

    stack = lambda items, i: jnp.stack([it[i] for it in items])
    return (xp.reshape(bsz, seq, d), xs.reshape(n_seq, n_tok, d),
            *(stack(new_kv_p, i) for i in range(4)),
            *(stack(new_kv_s, i) for i in range(4)))
```
